```python
import math
import jax, jax.numpy as jnp
from jax import lax
import numpy as np

D_MODEL = 4096
BATCH = 1
SEQ = 16384
DEPTH = 1
DEC_BATCH = 16
DEC_SEQ = 32
PAST_LEN = 2048

CHUNK = 64
HEAD_DIM = 128
D_SB = D_MODEL // 2
N_HEADS_SB = D_SB // HEAD_DIM
D_LRU = D_MODEL // 2
N_BLOCKS_LRU = 16
LRU_BLOCK = D_LRU // N_BLOCKS_LRU
D_MIX = D_SB + D_LRU
D_IN = 3 * D_SB + 2 * D_LRU
CONV_W_LRU = 4
LRU_C = 8.0
D_FF = 11008
CONV_W_FFN = 3
Q_BLOCK = 128
N_MOD = 6
EPS = 1e-6

kernel_name = 'hybrid_stickbreak_rglru_streaming_step'


def rms_norm(x, g):
    xf = x.astype(jnp.float32)
    y = xf * lax.rsqrt(jnp.mean(xf * xf, axis=-1, keepdims=True) + EPS) * g.astype(jnp.float32)
    return y.astype(x.dtype)


def causal_dwconv(x, prev, w, b):
    W = w.shape[0]
    T = x.shape[1]
    xp = jnp.concatenate([prev.astype(x.dtype), x], axis=1)
    y = b
    for j in range(W):
        y = y + xp[:, j:j + T] * w[j]
    return y, xp[:, T:]


def stick_breaking(q, k, v, q_pos, k_pos):
    z = jnp.einsum('bqhd,bkhd->bhqk', q.astype(jnp.float32), k.astype(jnp.float32)) * (HEAD_DIM ** -0.5)
    visible = k_pos[None, :] < q_pos[:, None]
    log_beta = jax.nn.log_sigmoid(z)
    log_keep = jnp.where(visible, jax.nn.log_sigmoid(-z), 0.0)
    shifted = jnp.pad(log_keep[..., 1:], ((0, 0), (0, 0), (0, 0), (0, 1)))
    log_pass = lax.cumsum(shifted, axis=3, reverse=True)
    weights = jnp.where(visible, jnp.exp(log_beta + log_pass), 0.0)
    return jnp.einsum('bhqk,bkhd->bqhd', weights, v.astype(jnp.float32))


def sb_prompt(q, k, v):
    B, S, H, Dh = q.shape
    nb = S // Q_BLOCK
    q_blocks = q.reshape(B, nb, Q_BLOCK, H, Dh).swapaxes(0, 1)
    starts = jnp.arange(nb, dtype=jnp.int32) * Q_BLOCK
    k_pos = jnp.arange(S, dtype=jnp.int32)

    def one(args):
        qb, s0 = args
        return stick_breaking(qb, k, v, s0 + jnp.arange(Q_BLOCK, dtype=jnp.int32), k_pos)

    o = lax.map(one, (q_blocks, starts))
    return o.swapaxes(0, 1).reshape(B, S, H, Dh)


def block_diag(x, w, b):
    xb = x.reshape(x.shape[0], x.shape[1], N_BLOCKS_LRU, LRU_BLOCK)
    y = jnp.einsum('btnc,ncd->btnd', xb, w.astype(jnp.float32))
    return y.reshape(x.shape) + b.astype(jnp.float32)


def rg_lru(x, h0, w_a, b_a, w_x, b_x, lam):
    xf = x.astype(jnp.float32)
    r = jax.nn.sigmoid(block_diag(xf, w_a, b_a))
    i = jax.nn.sigmoid(block_diag(xf, w_x, b_x))
    log_a = -LRU_C * r * jax.nn.softplus(-lam.astype(jnp.float32))
    a = jnp.exp(log_a)
    u = jnp.sqrt(-jnp.expm1(2.0 * log_a)) * (i * xf)

    def step(h, au):
        a_t, u_t = au
        h = a_t * h + u_t
        return h, h

    h_last, hs = lax.scan(step, h0.astype(jnp.float32), (a.swapaxes(0, 1), u.swapaxes(0, 1)))
    return hs.swapaxes(0, 1), h_last


def layer(x, c, p, past):
    k_past, v_past, conv_lru_prev, h_prev, conv_ffn_prev = past
    B, T, _ = x.shape
    mod = jax.nn.silu(c) @ p['w_ada'] + p['b_ada']
    sh_m, sc_m, g_m, sh_f, sc_f, g_f = jnp.split(mod[:, None, :], N_MOD, axis=-1)

    h = rms_norm(x, p['g_pre_mix']) * (1 + sc_m) + sh_m
    proj = h @ p['w_in']
    q, k, v, xl, gl = jnp.split(proj, [D_SB, 2 * D_SB, 3 * D_SB, 3 * D_SB + D_LRU], axis=-1)
    q = q.reshape(B, T, N_HEADS_SB, HEAD_DIM)
    k = k.reshape(B, T, N_HEADS_SB, HEAD_DIM)
    v = v.reshape(B, T, N_HEADS_SB, HEAD_DIM)
    if k_past is None:
        o_sb = sb_prompt(q, k, v)
    else:
        P = k_past.shape[1]
        k_all = jnp.concatenate([k_past.astype(k.dtype), k], axis=1)
        v_all = jnp.concatenate([v_past.astype(v.dtype), v], axis=1)
        o_sb = stick_breaking(q, k_all, v_all, P + jnp.arange(T, dtype=jnp.int32),
                              jnp.arange(P + T, dtype=jnp.int32))
    o_sb = o_sb.reshape(B, T, D_SB).astype(x.dtype)

    xc, conv_lru_new = causal_dwconv(xl, conv_lru_prev, p['w_conv_lru'], p['b_conv_lru'])
    hs, h_last = rg_lru(xc, h_prev, p['w_gate_a'], p['b_gate_a'], p['w_gate_x'], p['b_gate_x'], p['lru_lambda'])
    o_lru = hs.astype(x.dtype) * jax.nn.gelu(gl, approximate=True)

    mixed = jnp.concatenate([rms_norm(o_sb, p['g_grp_attn']), rms_norm(o_lru, p['g_grp_lru'])], axis=-1) @ p['w_out']
    x = x + g_m * rms_norm(mixed, p['g_post_mix'])

    h = rms_norm(x, p['g_pre_ffn']) * (1 + sc_f) + sh_f
    up, conv_ffn_new = causal_dwconv(h @ p['w_up'], conv_ffn_prev, p['w_conv_ffn'], p['b_conv_ffn'])
    gate, val = jnp.split(up, 2, axis=-1)
    y = (jax.nn.gelu(gate, approximate=True) * val) @ p['w_down']
    x = x + g_f * rms_norm(y, p['g_post_ffn'])
    return x, (k, v, conv_lru_new, h_last.astype(x.dtype), conv_ffn_new)


def setup_inputs(seed: int = 0) -> dict:
    key = jax.random.key(seed)
    ks = jax.random.split(key, 40)
    f32 = jnp.float32

    def nrm(i, shape, s):
        return jax.random.normal(ks[i], shape, f32) * s

    def gain(i, n):
        return 1.0 + 0.01 * jax.random.normal(ks[i], (DEPTH, n), f32)

    L = DEPTH
    u = jax.random.uniform(ks[39], (L, D_LRU), f32, minval=0.9, maxval=0.999)
    s = u ** (1.0 / LRU_C)
    lam = jnp.log(s) - jnp.log1p(-s)
    return {
        'x_prompt': nrm(0, (BATCH, SEQ, D_MODEL), 1.0),
        'x_sample': nrm(1, (DEC_BATCH, DEC_SEQ, D_MODEL), 1.0),
        'c_prompt': nrm(2, (BATCH, D_MODEL), 1.0),
        'c_sample': nrm(3, (DEC_BATCH, D_MODEL), 1.0),
        'cache_k': nrm(4, (L, DEC_BATCH, PAST_LEN, N_HEADS_SB, HEAD_DIM), 1.0),
        'cache_v': nrm(5, (L, DEC_BATCH, PAST_LEN, N_HEADS_SB, HEAD_DIM), 1.0),
        'state_conv_lru': nrm(6, (L, DEC_BATCH, CONV_W_LRU - 1, D_LRU), 1.0),
        'state_lru': nrm(7, (L, DEC_BATCH, D_LRU), 0.5),
        'state_conv_ffn': nrm(8, (L, DEC_BATCH, CONV_W_FFN - 1, 2 * D_FF), 1.0),
        'w_ada': nrm(9, (L, D_MODEL, N_MOD * D_MODEL), 0.5 * D_MODEL ** -0.5),
        'b_ada': nrm(10, (L, N_MOD * D_MODEL), 0.01),
        'g_pre_mix': gain(11, D_MODEL),
        'w_in': nrm(12, (L, D_MODEL, D_IN), D_MODEL ** -0.5),
        'w_conv_lru': nrm(13, (L, CONV_W_LRU, D_LRU), CONV_W_LRU ** -0.5),
        'b_conv_lru': nrm(14, (L, D_LRU), 0.01),
        'w_gate_a': nrm(15, (L, N_BLOCKS_LRU, LRU_BLOCK, LRU_BLOCK), LRU_BLOCK ** -0.5),
        'b_gate_a': nrm(16, (L, D_LRU), 0.01),
        'w_gate_x': nrm(17, (L, N_BLOCKS_LRU, LRU_BLOCK, LRU_BLOCK), LRU_BLOCK ** -0.5),
        'b_gate_x': nrm(18, (L, D_LRU), 0.01),
        'lru_lambda': lam,
        'g_grp_attn': gain(19, D_SB),
        'g_grp_lru': gain(20, D_LRU),
        'w_out': nrm(21, (L, D_MIX, D_MODEL), D_MIX ** -0.5),
        'g_post_mix': gain(22, D_MODEL),
        'g_pre_ffn': gain(23, D_MODEL),
        'w_up': nrm(24, (L, D_MODEL, 2 * D_FF), D_MODEL ** -0.5),
        'w_conv_ffn': nrm(25, (L, CONV_W_FFN, 2 * D_FF), CONV_W_FFN ** -0.5),
        'b_conv_ffn': nrm(26, (L, 2 * D_FF), 0.01),
        'w_down': nrm(27, (L, D_FF, D_MODEL), D_FF ** -0.5),
        'g_post_ffn': gain(28, D_MODEL),
    }


def reference(x_prompt, x_sample, c_prompt, c_sample, cache_k, cache_v, state_conv_lru, state_lru,
              state_conv_ffn, w_ada, b_ada, g_pre_mix, w_in, w_conv_lru, b_conv_lru, w_gate_a, b_gate_a,
              w_gate_x, b_gate_x, lru_lambda, g_grp_attn, g_grp_lru, w_out, g_post_mix, g_pre_ffn, w_up,
              w_conv_ffn, b_conv_ffn, w_down, g_post_ffn):
    xp, xs = x_prompt, x_sample
    Bp = xp.shape[0]
    st_p = [[], [], [], [], []]
    st_s = [[], [], [], [], []]
    for l in range(DEPTH):
        p = {
            'w_ada': w_ada[l], 'b_ada': b_ada[l], 'g_pre_mix': g_pre_mix[l], 'w_in': w_in[l],
            'w_conv_lru': w_conv_lru[l], 'b_conv_lru': b_conv_lru[l],
            'w_gate_a': w_gate_a[l], 'b_gate_a': b_gate_a[l], 'w_gate_x': w_gate_x[l], 'b_gate_x': b_gate_x[l],
            'lru_lambda': lru_lambda[l], 'g_grp_attn': g_grp_attn[l], 'g_grp_lru': g_grp_lru[l],
            'w_out': w_out[l], 'g_post_mix': g_post_mix[l], 'g_pre_ffn': g_pre_ffn[l], 'w_up': w_up[l],
            'w_conv_ffn': w_conv_ffn[l], 'b_conv_ffn': b_conv_ffn[l], 'w_down': w_down[l],
            'g_post_ffn': g_post_ffn[l],
        }
        prompt_past = (None, None,
                       jnp.zeros((Bp, CONV_W_LRU - 1, D_LRU), xp.dtype),
                       jnp.zeros((Bp, D_LRU), xp.dtype),
                       jnp.zeros((Bp, CONV_W_FFN - 1, 2 * D_FF), xp.dtype))
        xp, new_p = layer(xp, c_prompt, p, prompt_past)
        sample_past = (cache_k[l], cache_v[l], state_conv_lru[l], state_lru[l], state_conv_ffn[l])
        xs, new_s = layer(xs, c_sample, p, sample_past)
        for j in range(5):
            st_p[j].append(new_p[j])
            st_s[j].append(new_s[j])
    k_p, v_p, cl_p, h_p, cf_p = [jnp.stack(a, axis=0) for a in st_p]
    k_s, v_s, cl_s, h_s, cf_s = [jnp.stack(a, axis=0) for a in st_s]
    return (xp, xs, k_p, v_p, cl_p, h_p, cf_p, k_s, v_s, cl_s, h_s, cf_s)
```

```python
import functools
import math

import jax
import jax.numpy as jnp
from jax import lax
from jax.experimental import pallas as pl
from jax.experimental.pallas import tpu as pltpu

F32 = jnp.float32
BF16 = jnp.bfloat16

EPS = 1e-6
LRU_C = 8.0
HEAD_DIM = 128
LANES = 128
SUBLANES = 8
VMEM_LIMIT = 56 * 1024 * 1024

TM_ROW = 256
TM_MM = 512
TN_MM = 512
TN_ADA = 512
TM_FFN = 1024
TN_FFN = 256
TN_DOWN = 256
TT_LRU = 256
BQ_ATT = 256


def _pick(n, pref):
    if n <= pref:
        return n
    t = pref
    while n % t:
        t //= 2
    assert t >= SUBLANES, (n, pref)
    return t


def _cparams(*sem):
    return pltpu.CompilerParams(dimension_semantics=sem, vmem_limit_bytes=VMEM_LIMIT)


def _gelu_tanh(x):
    return 0.5 * x * (1.0 + jnp.tanh(math.sqrt(2.0 / math.pi) * (x + 0.044715 * (x * x * x))))


def _rms(x, g):
    ms = jnp.mean(x * x, axis=-1, keepdims=True)
    return x * lax.rsqrt(ms + EPS) * g


def _row_tile(m, *mods):
    per_row = any(mod.shape[0] != 1 for mod in mods)
    return _pick(m, TM_ROW // 2 if per_row else TM_ROW)


def _mod_spec(mod, tm, d):
    if mod.shape[0] == 1:
        return pl.BlockSpec((1, d), lambda i: (0, 0))
    return pl.BlockSpec((tm, d), lambda i: (i, 0))


def _ada_kernel(c_ref, w_ref, b_ref, o_ref):
    c = c_ref[...]
    a = (c * jax.nn.sigmoid(c)).astype(BF16)
    o_ref[...] = jnp.dot(a, w_ref[...].astype(BF16), preferred_element_type=F32) + b_ref[...]


def _ada(c, w, b):
    bsz, d = c.shape
    n = w.shape[1]
    tn = _pick(n, TN_ADA)
    return pl.pallas_call(
        _ada_kernel,
        grid=(n // tn,),
        in_specs=[pl.BlockSpec((bsz, d), lambda j: (0, 0)),
                  pl.BlockSpec((d, tn), lambda j: (0, j)),
                  pl.BlockSpec((1, tn), lambda j: (0, j))],
        out_specs=pl.BlockSpec((bsz, tn), lambda j: (0, j)),
        out_shape=jax.ShapeDtypeStruct((bsz, n), F32),
        compiler_params=_cparams("parallel"),
        name="ada_proj",
    )(c, w, b)


def _norm_mod_kernel(x_ref, g_ref, sc_ref, sh_ref, o_ref):
    y = _rms(x_ref[...], g_ref[...])
    o_ref[...] = (y * (1.0 + sc_ref[...]) + sh_ref[...]).astype(o_ref.dtype)


def _norm_mod(x, g, sc, sh):
    m, d = x.shape
    tm = _row_tile(m, sc, sh)
    return pl.pallas_call(
        _norm_mod_kernel,
        grid=(m // tm,),
        in_specs=[pl.BlockSpec((tm, d), lambda i: (i, 0)),
                  pl.BlockSpec((1, d), lambda i: (0, 0)),
                  _mod_spec(sc, tm, d), _mod_spec(sh, tm, d)],
        out_specs=pl.BlockSpec((tm, d), lambda i: (i, 0)),
        out_shape=jax.ShapeDtypeStruct((m, d), BF16),
        compiler_params=_cparams("parallel"),
        name="norm_mod",
    )(x, g, sc, sh)


def _group_norm_kernel(a_ref, b_ref, ga_ref, gb_ref, o_ref):
    da = a_ref.shape[1]
    o_ref[:, :da] = _rms(a_ref[...].astype(F32), ga_ref[...]).astype(o_ref.dtype)
    o_ref[:, da:] = _rms(b_ref[...].astype(F32), gb_ref[...]).astype(o_ref.dtype)


def _group_norm_concat(a, b, ga, gb):
    m, da = a.shape
    db = b.shape[1]
    tm = _pick(m, TM_ROW)
    return pl.pallas_call(
        _group_norm_kernel,
        grid=(m // tm,),
        in_specs=[pl.BlockSpec((tm, da), lambda i: (i, 0)),
                  pl.BlockSpec((tm, db), lambda i: (i, 0)),
                  pl.BlockSpec((1, da), lambda i: (0, 0)),
                  pl.BlockSpec((1, db), lambda i: (0, 0))],
        out_specs=pl.BlockSpec((tm, da + db), lambda i: (i, 0)),
        out_shape=jax.ShapeDtypeStruct((m, da + db), BF16),
        compiler_params=_cparams("parallel"),
        name="group_norm_concat",
    )(a, b, ga, gb)


def _post_mix_kernel(x_ref, y_ref, gpost_ref, gate_ref, gpre_ref, sc_ref, sh_ref, x1_ref, h_ref):
    x1 = x_ref[...] + gate_ref[...] * _rms(y_ref[...], gpost_ref[...])
    x1_ref[...] = x1
    h = _rms(x1, gpre_ref[...])
    h_ref[...] = (h * (1.0 + sc_ref[...]) + sh_ref[...]).astype(h_ref.dtype)


def _post_mix(x, y, gpost, gate, gpre, sc, sh):
    m, d = x.shape
    tm = _row_tile(m, gate, sc, sh)
    row = pl.BlockSpec((tm, d), lambda i: (i, 0))
    vec = pl.BlockSpec((1, d), lambda i: (0, 0))
    return pl.pallas_call(
        _post_mix_kernel,
        grid=(m // tm,),
        in_specs=[row, row, vec, _mod_spec(gate, tm, d), vec, _mod_spec(sc, tm, d), _mod_spec(sh, tm, d)],
        out_specs=[row, row],
        out_shape=[jax.ShapeDtypeStruct((m, d), F32), jax.ShapeDtypeStruct((m, d), BF16)],
        compiler_params=_cparams("parallel"),
        name="post_mix",
    )(x, y, gpost, gate, gpre, sc, sh)


def _post_ffn_kernel(x_ref, y_ref, gpost_ref, gate_ref, o_ref):
    o_ref[...] = x_ref[...] + gate_ref[...] * _rms(y_ref[...], gpost_ref[...])


def _post_ffn(x, y, gpost, gate):
    m, d = x.shape
    tm = _row_tile(m, gate)
    row = pl.BlockSpec((tm, d), lambda i: (i, 0))
    vec = pl.BlockSpec((1, d), lambda i: (0, 0))
    return pl.pallas_call(
        _post_ffn_kernel,
        grid=(m // tm,),
        in_specs=[row, row, vec, _mod_spec(gate, tm, d)],
        out_specs=row,
        out_shape=jax.ShapeDtypeStruct((m, d), F32),
        compiler_params=_cparams("parallel"),
        name="post_ffn",
    )(x, y, gpost, gate)


def _matmul_kernel(a_ref, b_ref, *o_refs):
    acc = jnp.dot(a_ref[...], b_ref[...], preferred_element_type=F32)
    for o_ref in o_refs:
        o_ref[...] = acc.astype(o_ref.dtype)


def _matmul(a, b, col0, n, out_dtypes, tn_pref=None, name="matmul"):
    m, k = a.shape
    tm = _pick(m, TM_MM)
    tn = _pick(n, TN_MM if tn_pref is None else tn_pref)
    assert col0 % tn == 0
    off = col0 // tn
    outs = pl.pallas_call(
        _matmul_kernel,
        grid=(m // tm, n // tn),
        in_specs=[pl.BlockSpec((tm, k), lambda i, j: (i, 0)),
                  pl.BlockSpec((k, tn), lambda i, j: (0, j + off))],
        out_specs=[pl.BlockSpec((tm, tn), lambda i, j: (i, j)) for _ in out_dtypes],
        out_shape=[jax.ShapeDtypeStruct((m, n), dt) for dt in out_dtypes],
        compiler_params=_cparams("parallel", "arbitrary"),
        name=name,
    )(a, b)
    return outs


def _upper_ones(cw):
    j = lax.broadcasted_iota(jnp.int32, (2 * cw, cw), 0)
    s = lax.broadcasted_iota(jnp.int32, (2 * cw, cw), 1)
    jj = jnp.where(j >= cw, j - cw, j)
    return jnp.where(jj > s, 1.0, 0.0).astype(BF16)


def _sb_tile(q, k, v, c, uu, masked):
    bq, dh = q.shape
    tk = k.shape[0]
    cw = uu.shape[1]
    nc = tk // cw
    z = lax.dot_general(q, k, (((1,), (1,)), ((), ())), preferred_element_type=F32) * (dh ** -0.5)
    l1p = jnp.log1p(jnp.exp(-jnp.abs(z)))
    log_beta = jnp.minimum(z, 0.0) - l1p
    log_keep = -jnp.maximum(z, 0.0) - l1p
    if masked:
        t_idx = lax.broadcasted_iota(jnp.int32, (bq, tk), 0)
        s_idx = lax.broadcasted_iota(jnp.int32, (bq, tk), 1)
        vis = s_idx < t_idx
        log_keep = jnp.where(vis, log_keep, 0.0)
    hi = log_keep.astype(BF16)
    lo = (log_keep - hi.astype(F32)).astype(BF16)
    xs = [jnp.concatenate([hi[:, ci * cw:(ci + 1) * cw], lo[:, ci * cw:(ci + 1) * cw]], axis=1)
          for ci in range(nc)]
    x = xs[0] if nc == 1 else jnp.concatenate(xs, axis=0)
    r = jnp.dot(x, uu, preferred_element_type=F32)
    ws = [None] * nc
    for ci in reversed(range(nc)):
        sl = slice(ci * cw, (ci + 1) * cw)
        w = jnp.exp(log_beta[:, sl] + r[ci * bq:(ci + 1) * bq] + c)
        if masked:
            w = jnp.where(vis[:, sl], w, 0.0)
        ws[ci] = w.astype(BF16)
        c = c + jnp.sum(log_keep[:, sl], axis=1, keepdims=True)
    w = ws[0] if nc == 1 else jnp.concatenate(ws, axis=1)
    return jnp.dot(w, v, preferred_element_type=F32), c


def _sb_prompt_kernel(q_ref, k_ref, v_ref, o_ref, *, bq):
    t_total = q_ref.shape[0]
    uu = _upper_ones(LANES)

    def q_block(qb, carry):
        q0 = pl.multiple_of(qb * bq, bq)
        q = q_ref[pl.ds(q0, bq), :]
        acc, c = _sb_tile(q, k_ref[pl.ds(q0, bq), :], v_ref[pl.ds(q0, bq), :],
                          jnp.zeros((bq, 1), F32), uu, True)

        def key_tile(i, ac):
            acc, c = ac
            k0 = pl.multiple_of((qb - 1 - i) * bq, bq)
            da, c = _sb_tile(q, k_ref[pl.ds(k0, bq), :], v_ref[pl.ds(k0, bq), :], c, uu, False)
            return acc + da, c

        acc, c = lax.fori_loop(0, qb, key_tile, (acc, c))
        o_ref[pl.ds(q0, bq), :] = acc.astype(o_ref.dtype)
        return carry

    lax.fori_loop(0, t_total // bq, q_block, 0)


def _sb_prompt(q, k, v):
    t, hd = q.shape
    h = hd // HEAD_DIM
    bq = _pick(t, BQ_ATT)
    spec = pl.BlockSpec((t, HEAD_DIM), lambda i: (0, i))
    return pl.pallas_call(
        functools.partial(_sb_prompt_kernel, bq=bq),
        grid=(h,),
        in_specs=[spec, spec, spec],
        out_specs=spec,
        out_shape=jax.ShapeDtypeStruct((t, hd), BF16),
        compiler_params=_cparams("parallel"),
        name="sb_prompt",
    )(q, k, v)


def _sb_sample_kernel(q_ref, kn_ref, vn_ref, kp_ref, vp_ref, o_ref):
    tq, dh = q_ref.shape
    uu = _upper_ones(LANES)
    q = q_ref[...]
    pad = jnp.zeros((LANES - tq, dh), BF16)
    kn = jnp.concatenate([kn_ref[...], pad], axis=0)
    vn = jnp.concatenate([vn_ref[...], pad], axis=0)
    acc, c = _sb_tile(q, kn, vn, jnp.zeros((tq, 1), F32), uu, True)
    da, _ = _sb_tile(q, kp_ref[0].astype(BF16), vp_ref[0].astype(BF16), c, uu, False)
    o_ref[...] = (acc + da).astype(o_ref.dtype)


def _sb_sample(q, kn, vn, kp, vp, tq):
    m, hd = q.shape
    bsz = m // tq
    h = hd // HEAD_DIM
    p = kp.shape[1]
    assert tq <= LANES and p % LANES == 0
    new = pl.BlockSpec((tq, HEAD_DIM), lambda b, i: (b, i))
    past = pl.BlockSpec((1, p, HEAD_DIM), lambda b, i: (b, 0, i))
    return pl.pallas_call(
        _sb_sample_kernel,
        grid=(bsz, h),
        in_specs=[new, new, new, past, past],
        out_specs=new,
        out_shape=jax.ShapeDtypeStruct((m, hd), BF16),
        compiler_params=_cparams("parallel", "parallel"),
        name="sb_sample",
    )(q, kn, vn, kp, vp)


def _shift_rows(x, prev8, s, row8):
    rolled = pltpu.roll(x, s, 0)
    first = jnp.where(row8 < s, pltpu.roll(prev8, s, 0), rolled[:SUBLANES])
    if x.shape[0] == SUBLANES:
        return first
    return jnp.concatenate([first, rolled[SUBLANES:]], axis=0)


def _causal_conv(x, prev8, w, b):
    width = w.shape[0]
    row8 = lax.broadcasted_iota(jnp.int32, (SUBLANES, x.shape[1]), 0)
    y = b + w[width - 1:width] * x
    for j in range(width - 1):
        y = y + w[j:j + 1] * _shift_rows(x, prev8, width - 1 - j, row8)
    return y


def _lru_kernel(xl_ref, gl_ref, prev_ref, h0_ref, cw_ref, cb_ref, wa_ref, ba_ref, wx_ref, bx_ref, lam_ref,
                o_ref, conv8_ref, h8_ref, prev_scr, h_scr, *, n_seq, seq_len):
    t = pl.program_id(1)

    @pl.when(t == 0)
    def _():
        prev_scr[...] = prev_ref[...]
        h_scr[...] = h0_ref[...]

    lam = lam_ref[...]
    softplus_neg_lam = jnp.maximum(-lam, 0.0) + jnp.log1p(jnp.exp(-jnp.abs(lam)))
    wa = wa_ref[0]
    wx = wx_ref[0]
    row = lax.broadcasted_iota(jnp.int32, (seq_len, LANES), 0)
    for s in range(n_seq):
        rows = slice(s * seq_len, (s + 1) * seq_len)
        x = xl_ref[rows, :]
        xc = _causal_conv(x, prev_scr[s], cw_ref[...], cb_ref[...])
        xcb = xc.astype(BF16)
        r = jax.nn.sigmoid(jnp.dot(xcb, wa, preferred_element_type=F32) + ba_ref[...])
        i = jax.nn.sigmoid(jnp.dot(xcb, wx, preferred_element_type=F32) + bx_ref[...])
        log_a = -LRU_C * r * softplus_neg_lam
        a = jnp.exp(log_a)
        u = jnp.sqrt(-jnp.tanh(log_a) * (a * a + 1.0)) * (i * xc)
        d = 1
        while d < seq_len:
            keep = row >= d
            a_sh = jnp.where(keep, pltpu.roll(a, d, 0), 1.0)
            u_sh = jnp.where(keep, pltpu.roll(u, d, 0), 0.0)
            u = u + a * u_sh
            a = a * a_sh
            d *= 2
        h_in = h_scr[s][SUBLANES - 1:SUBLANES, :]
        hs = a * h_in + u
        o_ref[rows, :] = (hs * _gelu_tanh(gl_ref[rows, :])).astype(o_ref.dtype)
        last_x = x[seq_len - SUBLANES:, :]
        last_h = hs[seq_len - SUBLANES:, :]
        prev_scr[s] = last_x
        h_scr[s] = last_h
        conv8_ref[s] = last_x
        h8_ref[s] = last_h


def _lru(xg, prev8, h08, conv_w, conv_b, w_a, b_a, w_x, b_x, lam, n_seq, seq_len):
    m, c2 = xg.shape
    c = c2 // 2
    nb = c // LANES
    assert w_a.shape == (nb, LANES, LANES) and seq_len % SUBLANES == 0 and seq_len >= conv_w.shape[0] - 1
    if n_seq == 1:
        tt = _pick(seq_len, TT_LRU)
        s_t, l_t = 1, tt
    else:
        tt = m
        s_t, l_t = n_seq, seq_len
    nt = m // tt
    width = conv_w.shape[0]
    vec = pl.BlockSpec((1, LANES), lambda n, t: (0, n))
    st8 = pl.BlockSpec((s_t, SUBLANES, LANES), lambda n, t: (0, 0, n))
    out8 = pl.BlockSpec((s_t, SUBLANES, LANES), lambda n, t: (t, 0, n))
    gate_w = pl.BlockSpec((1, LANES, LANES), lambda n, t: (n, 0, 0))
    return pl.pallas_call(
        functools.partial(_lru_kernel, n_seq=s_t, seq_len=l_t),
        grid=(nb, nt),
        in_specs=[pl.BlockSpec((tt, LANES), lambda n, t: (t, n)),
                  pl.BlockSpec((tt, LANES), lambda n, t: (t, n + nb)),
                  st8, st8,
                  pl.BlockSpec((width, LANES), lambda n, t: (0, n)), vec,
                  gate_w, vec, gate_w, vec, vec],
        out_specs=[pl.BlockSpec((tt, LANES), lambda n, t: (t, n)), out8, out8],
        out_shape=[jax.ShapeDtypeStruct((m, c), F32),
                   jax.ShapeDtypeStruct((nt * s_t, SUBLANES, c), F32),
                   jax.ShapeDtypeStruct((nt * s_t, SUBLANES, c), F32)],
        scratch_shapes=[pltpu.VMEM((s_t, SUBLANES, LANES), F32), pltpu.VMEM((s_t, SUBLANES, LANES), F32)],
        compiler_params=_cparams("parallel", "arbitrary"),
        name="rg_lru",
    )(xg, xg, prev8, h08, conv_w, conv_b, w_a, b_a, w_x, b_x, lam)


def _ffn_up_kernel(h_ref, wg_ref, wv_ref, pg_ref, pv_ref, cwg_ref, cwv_ref, cbg_ref, cbv_ref,
                   act_ref, lastg_ref, lastv_ref, carry_g, carry_v, *, n_seq, seq_len):
    i = pl.program_id(0)
    j = pl.program_id(1)

    @pl.when(i == 0)
    def _():
        carry_g[j] = pg_ref[...]
        carry_v[j] = pv_ref[...]

    h = h_ref[...]
    up_g = jnp.dot(h, wg_ref[...], preferred_element_type=F32)
    up_v = jnp.dot(h, wv_ref[...], preferred_element_type=F32)
    for s in range(n_seq):
        rows = slice(s * seq_len, (s + 1) * seq_len)
        xg = up_g[rows]
        xv = up_v[rows]
        yg = _causal_conv(xg, carry_g[j, s], cwg_ref[...], cbg_ref[...])
        yv = _causal_conv(xv, carry_v[j, s], cwv_ref[...], cbv_ref[...])
        act_ref[rows, :] = (_gelu_tanh(yg) * yv).astype(act_ref.dtype)
        last_g = xg[seq_len - SUBLANES:, :]
        last_v = xv[seq_len - SUBLANES:, :]
        carry_g[j, s] = last_g
        carry_v[j, s] = last_v
        lastg_ref[s] = last_g
        lastv_ref[s] = last_v


def _ffn_up(h, w_up, prev8, conv_w, conv_b, n_seq, seq_len):
    m, d = h.shape
    f = w_up.shape[1] // 2
    tn = _pick(f, TN_FFN)
    assert seq_len % SUBLANES == 0 and seq_len >= conv_w.shape[0] - 1
    if n_seq == 1:
        tm = _pick(seq_len, TM_FFN)
        s_t, l_t = 1, tm
    else:
        tm = m
        s_t, l_t = n_seq, seq_len
    ni, nj = m // tm, f // tn
    width = conv_w.shape[0]
    st8 = lambda off: pl.BlockSpec((s_t, SUBLANES, tn), lambda i, j: (0, 0, j + off))
    out8 = pl.BlockSpec((s_t, SUBLANES, tn), lambda i, j: (i, 0, j))
    wspec = lambda off: pl.BlockSpec((d, tn), lambda i, j: (0, j + off))
    cw = lambda off: pl.BlockSpec((width, tn), lambda i, j: (0, j + off))
    cb = lambda off: pl.BlockSpec((1, tn), lambda i, j: (0, j + off))
    return pl.pallas_call(
        functools.partial(_ffn_up_kernel, n_seq=s_t, seq_len=l_t),
        grid=(ni, nj),
        in_specs=[pl.BlockSpec((tm, d), lambda i, j: (i, 0)),
                  wspec(0), wspec(nj), st8(0), st8(nj), cw(0), cw(nj), cb(0), cb(nj)],
        out_specs=[pl.BlockSpec((tm, tn), lambda i, j: (i, j)), out8, out8],
        out_shape=[jax.ShapeDtypeStruct((m, f), BF16),
                   jax.ShapeDtypeStruct((ni * s_t, SUBLANES, f), F32),
                   jax.ShapeDtypeStruct((ni * s_t, SUBLANES, f), F32)],
        scratch_shapes=[pltpu.VMEM((nj, s_t, SUBLANES, tn), F32), pltpu.VMEM((nj, s_t, SUBLANES, tn), F32)],
        compiler_params=_cparams("arbitrary", "arbitrary"),
        name="ffn_up",
    )(h, w_up, w_up, prev8, prev8, conv_w, conv_w, conv_b, conv_b)


def _state8(state, n_rows):
    return jnp.pad(state.astype(F32), ((0, 0), (SUBLANES - n_rows, 0), (0, 0)))


def _layer(x, mods, p, past, n_seq, seq_len):
    m, d = x.shape
    d_sb = d // 2
    d_lru = d // 2
    sh_m, sc_m, g_m, sh_f, sc_f, g_f = mods
    k_past, v_past, conv_lru_prev, h_prev, conv_ffn_prev = past

    h = _norm_mod(x, p['g_pre_mix'], sc_m, sh_m)
    (q,) = _matmul(h, p['w_in'], 0, d_sb, (BF16,), name="proj_q")
    k32, k16 = _matmul(h, p['w_in'], d_sb, d_sb, (F32, BF16), name="proj_k")
    v32, v16 = _matmul(h, p['w_in'], 2 * d_sb, d_sb, (F32, BF16), name="proj_v")
    (xg,) = _matmul(h, p['w_in'], 3 * d_sb, 2 * d_lru, (F32,), name="proj_lru")

    if k_past is None:
        o_sb = _sb_prompt(q, k16, v16)
    else:
        o_sb = _sb_sample(q, k16, v16, k_past, v_past, seq_len)

    w_c = p['w_conv_lru']
    o_lru, conv8, h8 = _lru(xg, _state8(conv_lru_prev, w_c.shape[0] - 1),
                            jnp.broadcast_to(h_prev.astype(F32)[:, None, :], (n_seq, SUBLANES, d_lru)),
                            w_c, p['b_conv_lru'], p['w_gate_a'], p['b_gate_a'], p['w_gate_x'], p['b_gate_x'],
                            p['lru_lambda'], n_seq, seq_len)
    conv_lru_new = conv8[-n_seq:, SUBLANES - (w_c.shape[0] - 1):, :]
    h_last = h8[-n_seq:, SUBLANES - 1, :]

    mix_in = _group_norm_concat(o_sb, o_lru, p['g_grp_attn'], p['g_grp_lru'])
    (mixed,) = _matmul(mix_in, p['w_out'], 0, d, (F32,), name="proj_out")
    x1, h2 = _post_mix(x, mixed, p['g_post_mix'], g_m, p['g_pre_ffn'], sc_f, sh_f)

    w_f = p['w_conv_ffn']
    act, last_g, last_v = _ffn_up(h2, p['w_up'], _state8(conv_ffn_prev, w_f.shape[0] - 1),
                                  w_f, p['b_conv_ffn'], n_seq, seq_len)
    keep = SUBLANES - (w_f.shape[0] - 1)
    conv_ffn_new = jnp.concatenate([last_g[-n_seq:, keep:, :], last_v[-n_seq:, keep:, :]], axis=-1)
    (y,) = _matmul(act, p['w_down'], 0, d, (F32,), tn_pref=TN_DOWN, name="ffn_down")
    out = _post_ffn(x1, y, p['g_post_ffn'], g_f)
    return out, (k32, v32, conv_lru_new, h_last, conv_ffn_new)


def kernel(x_prompt, x_sample, c_prompt, c_sample, cache_k, cache_v, state_conv_lru, state_lru, state_conv_ffn, w_ada, b_ada, g_pre_mix, w_in, w_conv_lru, b_conv_lru, w_gate_a, b_gate_a, w_gate_x, b_gate_x, lru_lambda, g_grp_attn, g_grp_lru, w_out, g_post_mix, g_pre_ffn, w_up, w_conv_ffn, b_conv_ffn, w_down, g_post_ffn):
    depth = w_ada.shape[0]
    bp, tp, d = x_prompt.shape
    bs, ts, _ = x_sample.shape
    assert bp == 1, "the prompt kernels treat the prompt as one sequence"
    d_sb = d // 2
    n_heads = d_sb // HEAD_DIM
    n_mod = w_ada.shape[2] // d

    xp = x_prompt.reshape(bp * tp, d)
    xs = x_sample.reshape(bs * ts, d)
    n_c = bp + bs
    n_c_pad = -(-n_c // SUBLANES) * SUBLANES
    c_all = jnp.concatenate([c_prompt, c_sample, jnp.zeros((n_c_pad - n_c, d), F32)], axis=0)

    st_p = [[], [], [], [], []]
    st_s = [[], [], [], [], []]
    for l in range(depth):
        row = lambda a: a[l][None, :]
        p = {
            'g_pre_mix': row(g_pre_mix), 'w_in': w_in[l].astype(BF16),
            'w_conv_lru': w_conv_lru[l], 'b_conv_lru': row(b_conv_lru),
            'w_gate_a': w_gate_a[l].astype(BF16), 'b_gate_a': row(b_gate_a),
            'w_gate_x': w_gate_x[l].astype(BF16), 'b_gate_x': row(b_gate_x),
            'lru_lambda': row(lru_lambda), 'g_grp_attn': row(g_grp_attn), 'g_grp_lru': row(g_grp_lru),
            'w_out': w_out[l].astype(BF16), 'g_post_mix': row(g_post_mix), 'g_pre_ffn': row(g_pre_ffn),
            'w_up': w_up[l].astype(BF16), 'w_conv_ffn': w_conv_ffn[l], 'b_conv_ffn': row(b_conv_ffn),
            'w_down': w_down[l].astype(BF16), 'g_post_ffn': row(g_post_ffn),
        }
        mod = _ada(c_all, w_ada[l], row(b_ada))
        mods_p = [mod[0:1, i * d:(i + 1) * d] for i in range(n_mod)]
        mods_s = [jnp.repeat(mod[bp:bp + bs, i * d:(i + 1) * d], ts, axis=0) for i in range(n_mod)]

        d_lru = d // 2
        f2 = w_up.shape[2]
        prompt_past = (None, None, jnp.zeros((bp, w_conv_lru.shape[1] - 1, d_lru), F32),
                       jnp.zeros((bp, d_lru), F32), jnp.zeros((bp, w_conv_ffn.shape[1] - 1, f2), F32))
        xp, new_p = _layer(xp, mods_p, p, prompt_past, bp, tp)
        past_len = cache_k.shape[2]
        sample_past = (cache_k[l].reshape(bs, past_len, d_sb), cache_v[l].reshape(bs, past_len, d_sb),
                       state_conv_lru[l], state_lru[l], state_conv_ffn[l])
        xs, new_s = _layer(xs, mods_s, p, sample_past, bs, ts)
        for j in range(5):
            st_p[j].append(new_p[j])
            st_s[j].append(new_s[j])

    def pack(st, b, t):
        k = jnp.stack(st[0], axis=0).reshape(depth, b, t, n_heads, HEAD_DIM)
        v = jnp.stack(st[1], axis=0).reshape(depth, b, t, n_heads, HEAD_DIM)
        return k, v, jnp.stack(st[2], axis=0), jnp.stack(st[3], axis=0), jnp.stack(st[4], axis=0)

    k_p, v_p, cl_p, h_p, cf_p = pack(st_p, bp, tp)
    k_s, v_s, cl_s, h_s, cf_s = pack(st_s, bs, ts)
    return (xp.reshape(bp, tp, d), xs.reshape(bs, ts, d), k_p, v_p, cl_p, h_p, cf_p, k_s, v_s, cl_s, h_s, cf_s)
```

```python
import functools
import math

import jax
import jax.numpy as jnp
from jax import lax
from jax.experimental import pallas as pl
from jax.experimental.pallas import tpu as pltpu

F32 = jnp.float32
BF16 = jnp.bfloat16

EPS = 1e-6
LRU_C = 8.0
HEAD_DIM = 128
LANES = 128
SUBLANES = 8
VMEM_LIMIT = 56 * 1024 * 1024
LOG_WEIGHT_ZERO = -110.0

TM_ROW = 256
TM_MM = 1024
TN_MM = 1024
TN_ADA = 512
TM_FFN = 1024
TN_FFN = 256
PIECE_FFN = 128
TM_DOWN = 512
TN_DOWN = 256
TT_LRU = 1024
PC_LRU = 256
BQ_ATT = 256
TK_SAMPLE = 512


def _pick(n, pref):
    if n <= pref:
        return n
    t = pref
    while n % t:
        t //= 2
    assert t >= SUBLANES, (n, pref)
    return t


def _cparams(*sem):
    return pltpu.CompilerParams(dimension_semantics=sem, vmem_limit_bytes=VMEM_LIMIT)


def _gelu_tanh(x):
    return 0.5 * x * (1.0 + jnp.tanh(math.sqrt(2.0 / math.pi) * (x + 0.044715 * (x * x * x))))


def _rms(x, g):
    ms = jnp.mean(x * x, axis=-1, keepdims=True)
    return x * lax.rsqrt(ms + EPS) * g


def _row_tile(m, *mods):
    per_row = any(mod.shape[0] != 1 for mod in mods)
    return _pick(m, TM_ROW // 2 if per_row else TM_ROW)


def _mod_spec(mod, tm, d):
    if mod.shape[0] == 1:
        return pl.BlockSpec((1, d), lambda i: (0, 0))
    return pl.BlockSpec((tm, d), lambda i: (i, 0))


def _ada_kernel(c_ref, w_ref, b_ref, o_ref):
    c = c_ref[...]
    a = (c * jax.nn.sigmoid(c)).astype(BF16)
    o_ref[...] = jnp.dot(a, w_ref[...].astype(BF16), preferred_element_type=F32) + b_ref[...]


def _ada(c, w, b):
    bsz, d = c.shape
    n = w.shape[1]
    tn = _pick(n, TN_ADA)
    return pl.pallas_call(
        _ada_kernel,
        grid=(n // tn,),
        in_specs=[pl.BlockSpec((bsz, d), lambda j: (0, 0)),
                  pl.BlockSpec((d, tn), lambda j: (0, j)),
                  pl.BlockSpec((1, tn), lambda j: (0, j))],
        out_specs=pl.BlockSpec((bsz, tn), lambda j: (0, j)),
        out_shape=jax.ShapeDtypeStruct((bsz, n), F32),
        compiler_params=_cparams("parallel"),
        name="ada_proj",
    )(c, w, b)


def _norm_mod_kernel(x_ref, g_ref, sc_ref, sh_ref, o_ref):
    y = _rms(x_ref[...], g_ref[...])
    o_ref[...] = (y * (1.0 + sc_ref[...]) + sh_ref[...]).astype(o_ref.dtype)


def _norm_mod(x, g, sc, sh):
    m, d = x.shape
    tm = _row_tile(m, sc, sh)
    return pl.pallas_call(
        _norm_mod_kernel,
        grid=(m // tm,),
        in_specs=[pl.BlockSpec((tm, d), lambda i: (i, 0)),
                  pl.BlockSpec((1, d), lambda i: (0, 0)),
                  _mod_spec(sc, tm, d), _mod_spec(sh, tm, d)],
        out_specs=pl.BlockSpec((tm, d), lambda i: (i, 0)),
        out_shape=jax.ShapeDtypeStruct((m, d), BF16),
        compiler_params=_cparams("parallel"),
        name="norm_mod",
    )(x, g, sc, sh)


def _group_norm_kernel(a_ref, b_ref, ga_ref, gb_ref, o_ref):
    a = jnp.concatenate([a_ref[h] for h in range(a_ref.shape[0])], axis=1).astype(F32)
    da = a.shape[1]
    o_ref[:, :da] = _rms(a, ga_ref[...]).astype(o_ref.dtype)
    o_ref[:, da:] = _rms(b_ref[...].astype(F32), gb_ref[...]).astype(o_ref.dtype)


def _group_norm_concat(a, b, ga, gb):
    h, m, dh = a.shape
    da = h * dh
    db = b.shape[1]
    tm = _pick(m, TM_ROW)
    return pl.pallas_call(
        _group_norm_kernel,
        grid=(m // tm,),
        in_specs=[pl.BlockSpec((h, tm, dh), lambda i: (0, i, 0)),
                  pl.BlockSpec((tm, db), lambda i: (i, 0)),
                  pl.BlockSpec((1, da), lambda i: (0, 0)),
                  pl.BlockSpec((1, db), lambda i: (0, 0))],
        out_specs=pl.BlockSpec((tm, da + db), lambda i: (i, 0)),
        out_shape=jax.ShapeDtypeStruct((m, da + db), BF16),
        compiler_params=_cparams("parallel"),
        name="group_norm_concat",
    )(a, b, ga, gb)


def _post_mix_kernel(x_ref, y_ref, gpost_ref, gate_ref, gpre_ref, sc_ref, sh_ref, x1_ref, h_ref):
    x1 = x_ref[...] + gate_ref[...] * _rms(y_ref[...], gpost_ref[...])
    x1_ref[...] = x1
    h = _rms(x1, gpre_ref[...])
    h_ref[...] = (h * (1.0 + sc_ref[...]) + sh_ref[...]).astype(h_ref.dtype)


def _post_mix(x, y, gpost, gate, gpre, sc, sh):
    m, d = x.shape
    tm = _row_tile(m, gate, sc, sh)
    row = pl.BlockSpec((tm, d), lambda i: (i, 0))
    vec = pl.BlockSpec((1, d), lambda i: (0, 0))
    return pl.pallas_call(
        _post_mix_kernel,
        grid=(m // tm,),
        in_specs=[row, row, vec, _mod_spec(gate, tm, d), vec, _mod_spec(sc, tm, d), _mod_spec(sh, tm, d)],
        out_specs=[row, row],
        out_shape=[jax.ShapeDtypeStruct((m, d), F32), jax.ShapeDtypeStruct((m, d), BF16)],
        compiler_params=_cparams("parallel"),
        name="post_mix",
    )(x, y, gpost, gate, gpre, sc, sh)


def _post_ffn_kernel(x_ref, y_ref, gpost_ref, gate_ref, o_ref):
    o_ref[...] = x_ref[...] + gate_ref[...] * _rms(y_ref[...], gpost_ref[...])


def _post_ffn(x, y, gpost, gate):
    m, d = x.shape
    tm = _row_tile(m, gate)
    row = pl.BlockSpec((tm, d), lambda i: (i, 0))
    vec = pl.BlockSpec((1, d), lambda i: (0, 0))
    return pl.pallas_call(
        _post_ffn_kernel,
        grid=(m // tm,),
        in_specs=[row, row, vec, _mod_spec(gate, tm, d)],
        out_specs=row,
        out_shape=jax.ShapeDtypeStruct((m, d), F32),
        compiler_params=_cparams("parallel"),
        name="post_ffn",
    )(x, y, gpost, gate)


def _matmul_kernel(a_ref, b_ref, *o_refs):
    acc = jnp.dot(a_ref[...], b_ref[...], preferred_element_type=F32)
    for o_ref in o_refs:
        if len(o_ref.shape) == 3:
            for h in range(o_ref.shape[0]):
                o_ref[h] = acc[:, h * LANES:(h + 1) * LANES].astype(o_ref.dtype)
        else:
            o_ref[...] = acc.astype(o_ref.dtype)


def _matmul(a, b, col0, n, outs, tm_pref=None, tn_pref=None, name="matmul"):
    m, k = a.shape
    tm = _pick(m, TM_MM if tm_pref is None else tm_pref)
    tn = _pick(n, TN_MM if tn_pref is None else tn_pref)
    assert col0 % tn == 0 and tn % LANES == 0
    off = col0 // tn
    hpt = tn // LANES
    out_specs, out_shape = [], []
    for dt, head_major in outs:
        if head_major:
            out_specs.append(pl.BlockSpec((hpt, tm, LANES), lambda i, j: (j, i, 0)))
            out_shape.append(jax.ShapeDtypeStruct((n // LANES, m, LANES), dt))
        else:
            out_specs.append(pl.BlockSpec((tm, tn), lambda i, j: (i, j)))
            out_shape.append(jax.ShapeDtypeStruct((m, n), dt))
    return pl.pallas_call(
        _matmul_kernel,
        grid=(m // tm, n // tn),
        in_specs=[pl.BlockSpec((tm, k), lambda i, j: (i, 0)),
                  pl.BlockSpec((k, tn), lambda i, j: (0, j + off))],
        out_specs=out_specs,
        out_shape=out_shape,
        compiler_params=_cparams("parallel", "arbitrary"),
        name=name,
    )(a, b)


def _cumsum_ones(cw):
    j = lax.broadcasted_iota(jnp.int32, (2 * cw, 2 * cw), 0)
    s = lax.broadcasted_iota(jnp.int32, (2 * cw, 2 * cw), 1)
    jj = jnp.where(j >= cw, j - cw, j)
    return jnp.where((s >= cw) | (jj > s), 1.0, 0.0).astype(BF16)


def _sb_tile(q, k, v, c, uu, mask_offset):
    return _sb_tiles([(q, k, v, c)], uu, mask_offset)[0]


def _sb_tiles(items, uu, mask_offset):
    scored = [_sb_scores(q, k, uu.shape[0] // 2, mask_offset) for q, k, _, _ in items]
    xs = [x for _, _, x in scored]
    r = jnp.dot(xs[0] if len(xs) == 1 else jnp.concatenate(xs, axis=0), uu, preferred_element_type=F32)
    outs, row = [], 0
    for (_, _, v, c), (log_betas, vis, x) in zip(items, scored):
        outs.append(_sb_weights(log_betas, vis, r[row:row + x.shape[0]], c, v))
        row += x.shape[0]
    return outs


def _sb_scores(q, k, cw, mask_offset):
    bq, dh = q.shape
    nc = k.shape[0] // cw
    z = lax.dot_general(q, k, (((1,), (1,)), ((), ())), preferred_element_type=F32) * (dh ** -0.5)
    log_betas, vis, xs = [], [], []
    for ci in range(nc):
        zc = z[:, ci * cw:(ci + 1) * cw]
        log_beta = jnp.minimum(zc, 0.0) - jnp.log(1.0 + jnp.exp(-jnp.abs(zc)))
        log_keep = log_beta - zc
        if mask_offset is not None and (ci + 1) * cw > mask_offset:
            t_idx = lax.broadcasted_iota(jnp.int32, (bq, cw), 0)
            s_idx = lax.broadcasted_iota(jnp.int32, (bq, cw), 1)
            vis_c = s_idx < t_idx + (mask_offset - ci * cw)
            log_keep = jnp.where(vis_c, log_keep, 0.0)
        else:
            vis_c = None
        hi = log_keep.astype(BF16)
        lo = (log_keep - hi.astype(F32)).astype(BF16)
        log_betas.append(log_beta)
        vis.append(vis_c)
        xs.append(jnp.concatenate([hi, lo], axis=1))
    return log_betas, vis, xs[0] if nc == 1 else jnp.concatenate(xs, axis=0)


def _sb_weights(log_betas, vis, r, c, v):
    nc = len(log_betas)
    bq, cw = log_betas[0].shape
    ws = [None] * nc
    for ci in reversed(range(nc)):
        rc = r[ci * bq:(ci + 1) * bq]
        w = jnp.exp(log_betas[ci] + rc[:, :cw] + c)
        if vis[ci] is not None:
            w = jnp.where(vis[ci], w, 0.0)
        ws[ci] = w.astype(BF16)
        c = c + rc[:, cw:]
    w = ws[0] if nc == 1 else jnp.concatenate(ws, axis=1)
    return jnp.dot(w, v, preferred_element_type=F32), c


def _sb_prompt_kernel(q_ref, k_ref, v_ref, o_ref, acc_scr, c_scr, *, bq, n_par):
    n_blocks = q_ref.shape[0] // bq
    n_groups = (n_blocks - 1) // n_par
    uu = _cumsum_ones(LANES)
    c0 = jnp.zeros((bq, LANES), F32)

    acc, _ = _sb_tile(q_ref[0:bq, :], k_ref[0:bq, :], v_ref[0:bq, :], c0, uu, 0)
    o_ref[0:bq, :] = acc.astype(o_ref.dtype)

    def group(g, carry):
        items, q0s = [], []
        for u in range(n_par):
            q0 = pl.multiple_of((1 + g + u * n_groups) * bq, bq)
            k0 = pl.multiple_of(q0 - bq, bq)
            items.append((q_ref[pl.ds(q0, bq), :], k_ref[pl.ds(k0, 2 * bq), :], v_ref[pl.ds(k0, 2 * bq), :], c0))
            q0s.append(q0)
        res = _sb_tiles(items, uu, bq)
        for q0, (acc, _) in zip(q0s, res):
            o_ref[pl.ds(q0, bq), :] = acc.astype(o_ref.dtype)
        c_max = res[0][1]
        for _, c in res[1:]:
            c_max = jnp.maximum(c_max, c)

        @pl.when(jnp.max(c_max) > LOG_WEIGHT_ZERO)
        def _():
            for u, (acc, c) in enumerate(res):
                acc_scr[u] = acc
                c_scr[u] = c

            def older_keys(u, carry):
                qb = 1 + g + u * n_groups
                q0 = pl.multiple_of(qb * bq, bq)
                k0 = pl.multiple_of(q0 - bq, bq)
                q = q_ref[pl.ds(q0, bq), :]
                n_wide = (qb - 1) // 2

                def more(st):
                    j, _, c = st
                    return jnp.logical_and(j < n_wide, jnp.max(c) > LOG_WEIGHT_ZERO)

                def wide_tile(st):
                    j, acc, c = st
                    ks = pl.multiple_of(k0 - 2 * bq * (j + 1), bq)
                    da, c = _sb_tile(q, k_ref[pl.ds(ks, 2 * bq), :], v_ref[pl.ds(ks, 2 * bq), :], c, uu, None)
                    return j + 1, acc + da, c

                j, acc, c = lax.while_loop(more, wide_tile, (jnp.int32(0), acc_scr[u], c_scr[u]))
                need_last = jnp.logical_and(jnp.logical_and((qb - 1) % 2 == 1, j == n_wide),
                                            jnp.max(c) > LOG_WEIGHT_ZERO)

                def last_tile(acc, c):
                    da, _ = _sb_tile(q, k_ref[0:bq, :], v_ref[0:bq, :], c, uu, None)
                    return acc + da

                acc = lax.cond(need_last, last_tile, lambda acc, c: acc, acc, c)
                o_ref[pl.ds(q0, bq), :] = acc.astype(o_ref.dtype)
                return carry

            lax.fori_loop(0, n_par, older_keys, 0)

        return carry

    lax.fori_loop(0, n_groups, group, 0)


def _sb_prompt(q, k, v):
    h, t, dh = q.shape
    bq = _pick(t, BQ_ATT)
    n_rest = t // bq - 1
    n_par = next(u for u in (7, 4, 3, 2, 1) if n_rest % u == 0)
    spec = pl.BlockSpec((None, t, dh), lambda i: (i, 0, 0))
    return pl.pallas_call(
        functools.partial(_sb_prompt_kernel, bq=bq, n_par=n_par),
        grid=(h,),
        in_specs=[spec, spec, spec],
        out_specs=spec,
        out_shape=jax.ShapeDtypeStruct((h, t, dh), BF16),
        scratch_shapes=[pltpu.VMEM((n_par, bq, dh), F32), pltpu.VMEM((n_par, bq, LANES), F32)],
        compiler_params=_cparams("parallel"),
        name="sb_prompt",
    )(q, k, v)


def _sb_sample_kernel(q_ref, kn_ref, vn_ref, kp_ref, vp_ref, o_ref, acc_scr, c_scr, *, tk):
    n_heads, tq, dh = q_ref.shape
    kt = pl.program_id(1)
    uu = _cumsum_ones(LANES)

    @pl.when(kt == 0)
    def _():
        pad = jnp.zeros((LANES - tq, dh), BF16)
        c0 = jnp.zeros((tq, LANES), F32)
        items = [(q_ref[h], jnp.concatenate([kn_ref[h], pad], axis=0),
                  jnp.concatenate([vn_ref[h], pad], axis=0), c0) for h in range(n_heads)]
        for h, (acc, c) in enumerate(_sb_tiles(items, uu, 0)):
            acc_scr[h] = acc
            c_scr[h] = c

    items = [(q_ref[h], kp_ref[0, pl.ds(h, tk, stride=n_heads), :].astype(BF16),
              vp_ref[0, pl.ds(h, tk, stride=n_heads), :].astype(BF16), c_scr[h]) for h in range(n_heads)]
    for h, (da, c) in enumerate(_sb_tiles(items, uu, None)):
        acc_scr[h] = acc_scr[h] + da
        c_scr[h] = c

    @pl.when(kt == pl.num_programs(1) - 1)
    def _():
        for h in range(n_heads):
            o_ref[h] = acc_scr[h].astype(o_ref.dtype)


def _sb_sample(q, kn, vn, kp, vp, tq):
    h, m, dh = q.shape
    bsz = m // tq
    p = kp.shape[1] // h
    tk = _pick(p, TK_SAMPLE)
    n_kt = p // tk
    assert tq <= LANES and tk % LANES == 0
    new = pl.BlockSpec((h, tq, dh), lambda b, kt: (0, b, 0))
    past = pl.BlockSpec((1, tk * h, dh), lambda b, kt: (b, n_kt - 1 - kt, 0))
    return pl.pallas_call(
        functools.partial(_sb_sample_kernel, tk=tk),
        grid=(bsz, n_kt),
        in_specs=[new, new, new, past, past],
        out_specs=new,
        out_shape=jax.ShapeDtypeStruct((h, m, dh), BF16),
        scratch_shapes=[pltpu.VMEM((h, tq, dh), F32), pltpu.VMEM((h, tq, LANES), F32)],
        compiler_params=_cparams("parallel", "arbitrary"),
        name="sb_sample",
    )(q, kn, vn, kp, vp)


def _shift_rows(x, prev8, s, row8):
    rolled = pltpu.roll(x, s, 0)
    first = jnp.where(row8 < s, pltpu.roll(prev8, s, 0), rolled[:SUBLANES])
    if x.shape[0] == SUBLANES:
        return first
    return jnp.concatenate([first, rolled[SUBLANES:]], axis=0)


def _causal_conv(x, prev8, w, b):
    width = w.shape[0]
    row8 = lax.broadcasted_iota(jnp.int32, (SUBLANES, x.shape[1]), 0)
    y = b + w[width - 1:width] * x
    for j in range(width - 1):
        y = y + w[j:j + 1] * _shift_rows(x, prev8, width - 1 - j, row8)
    return y


def _linear_scan(a, u, h_in):
    n = a.shape[0]
    sub = jnp.bitwise_and(lax.broadcasted_iota(jnp.int32, a.shape, 0), SUBLANES - 1)
    d = 1
    while d < SUBLANES:
        keep = sub >= d
        a_sh = jnp.where(keep, pltpu.roll(a, d, 0), 1.0)
        u_sh = jnp.where(keep, pltpu.roll(u, d, 0), 0.0)
        u = u + a * u_sh
        a = a * a_sh
        d *= 2
    hs = []
    for g in range(n // SUBLANES):
        rows = slice(g * SUBLANES, (g + 1) * SUBLANES)
        hg = a[rows] * h_in + u[rows]
        hs.append(hg)
        h_in = hg[SUBLANES - 1:SUBLANES, :]
    return hs[0] if len(hs) == 1 else jnp.concatenate(hs, axis=0)


def _lru_kernel(xl_ref, gl_ref, prev_ref, h0_ref, cw_ref, cb_ref, wa_ref, ba_ref, wx_ref, bx_ref, lam_ref,
                o_ref, conv8_ref, h8_ref, prev_scr, h_scr, *, n_seq, piece):
    t = pl.program_id(1)

    @pl.when(t == 0)
    def _():
        prev_scr[...] = prev_ref[...]
        h_scr[...] = h0_ref[...]

    lam = lam_ref[...]
    softplus_neg_lam = jnp.maximum(-lam, 0.0) + jnp.log1p(jnp.exp(-jnp.abs(lam)))
    wa = wa_ref[0]
    wx = wx_ref[0]
    seq_rows = xl_ref.shape[0] // n_seq
    for s in range(n_seq):
        prev8 = prev_scr[s]
        h8 = h_scr[s]
        for pc in range(seq_rows // piece):
            rows = slice(s * seq_rows + pc * piece, s * seq_rows + (pc + 1) * piece)
            x = xl_ref[rows, :]
            xc = _causal_conv(x, prev8, cw_ref[...], cb_ref[...])
            xcb = xc.astype(BF16)
            r = jax.nn.sigmoid(jnp.dot(xcb, wa, preferred_element_type=F32) + ba_ref[...])
            i = jax.nn.sigmoid(jnp.dot(xcb, wx, preferred_element_type=F32) + bx_ref[...])
            log_a = -LRU_C * r * softplus_neg_lam
            a = jnp.exp(log_a)
            u = jnp.sqrt(-jnp.tanh(log_a) * (a * a + 1.0)) * (i * xc)
            hs = _linear_scan(a, u, h8[SUBLANES - 1:SUBLANES, :])
            o_ref[rows, :] = (hs * _gelu_tanh(gl_ref[rows, :])).astype(o_ref.dtype)
            prev8 = x[piece - SUBLANES:, :]
            h8 = hs[piece - SUBLANES:, :]
        prev_scr[s] = prev8
        h_scr[s] = h8
        conv8_ref[s] = prev8
        h8_ref[s] = h8


def _lru(xg, prev8, h08, conv_w, conv_b, w_a, b_a, w_x, b_x, lam, n_seq, seq_len):
    m, c2 = xg.shape
    c = c2 // 2
    nb = c // LANES
    assert w_a.shape == (nb, LANES, LANES) and seq_len % SUBLANES == 0 and seq_len >= conv_w.shape[0] - 1
    if n_seq == 1:
        tt = _pick(seq_len, TT_LRU)
        s_t, piece = 1, _pick(tt, PC_LRU)
    else:
        tt = m
        s_t, piece = n_seq, seq_len
    nt = m // tt
    width = conv_w.shape[0]
    vec = pl.BlockSpec((1, LANES), lambda n, t: (0, n))
    st8 = pl.BlockSpec((s_t, SUBLANES, LANES), lambda n, t: (0, 0, n))
    out8 = pl.BlockSpec((s_t, SUBLANES, LANES), lambda n, t: (t, 0, n))
    gate_w = pl.BlockSpec((1, LANES, LANES), lambda n, t: (n, 0, 0))
    return pl.pallas_call(
        functools.partial(_lru_kernel, n_seq=s_t, piece=piece),
        grid=(nb, nt),
        in_specs=[pl.BlockSpec((tt, LANES), lambda n, t: (t, n)),
                  pl.BlockSpec((tt, LANES), lambda n, t: (t, n + nb)),
                  st8, st8,
                  pl.BlockSpec((width, LANES), lambda n, t: (0, n)), vec,
                  gate_w, vec, gate_w, vec, vec],
        out_specs=[pl.BlockSpec((tt, LANES), lambda n, t: (t, n)), out8, out8],
        out_shape=[jax.ShapeDtypeStruct((m, c), F32),
                   jax.ShapeDtypeStruct((nt * s_t, SUBLANES, c), F32),
                   jax.ShapeDtypeStruct((nt * s_t, SUBLANES, c), F32)],
        scratch_shapes=[pltpu.VMEM((s_t, SUBLANES, LANES), F32), pltpu.VMEM((s_t, SUBLANES, LANES), F32)],
        compiler_params=_cparams("parallel", "arbitrary"),
        name="rg_lru",
    )(xg, xg, prev8, h08, conv_w, conv_b, w_a, b_a, w_x, b_x, lam)


def _ffn_up_kernel(h_ref, wg_ref, wv_ref, pg_ref, pv_ref, cwg_ref, cwv_ref, cbg_ref, cbv_ref,
                   act_ref, lastg_ref, lastv_ref, carry_g, carry_v, *, n_seq):
    i = pl.program_id(0)
    j = pl.program_id(1)

    @pl.when(i == 0)
    def _():
        carry_g[j] = pg_ref[...]
        carry_v[j] = pv_ref[...]

    tm = h_ref.shape[0]
    seq_rows = tm // n_seq
    piece = min(seq_rows, PIECE_FFN)
    h = h_ref[...]
    up_g = jnp.dot(h, wg_ref[...], preferred_element_type=F32)
    up_v = jnp.dot(h, wv_ref[...], preferred_element_type=F32)
    for s in range(n_seq):
        prev_g = carry_g[j, s]
        prev_v = carry_v[j, s]
        for p0 in range(s * seq_rows, (s + 1) * seq_rows, piece):
            xg = up_g[p0:p0 + piece]
            xv = up_v[p0:p0 + piece]
            yg = _causal_conv(xg, prev_g, cwg_ref[...], cbg_ref[...])
            yv = _causal_conv(xv, prev_v, cwv_ref[...], cbv_ref[...])
            act_ref[p0:p0 + piece, :] = (_gelu_tanh(yg) * yv).astype(act_ref.dtype)
            prev_g = xg[piece - SUBLANES:, :]
            prev_v = xv[piece - SUBLANES:, :]
        carry_g[j, s] = prev_g
        carry_v[j, s] = prev_v
        lastg_ref[s] = prev_g
        lastv_ref[s] = prev_v


def _ffn_up(h, w_up, prev8, conv_w, conv_b, n_seq, seq_len):
    m, d = h.shape
    f = w_up.shape[1] // 2
    tn = _pick(f, TN_FFN)
    assert seq_len % SUBLANES == 0 and seq_len >= conv_w.shape[0] - 1
    if n_seq == 1:
        tm = _pick(seq_len, TM_FFN)
        s_t = 1
    else:
        tm = m
        s_t = n_seq
    ni, nj = m // tm, f // tn
    width = conv_w.shape[0]
    st8 = lambda off: pl.BlockSpec((s_t, SUBLANES, tn), lambda i, j: (0, 0, j + off))
    out8 = pl.BlockSpec((s_t, SUBLANES, tn), lambda i, j: (i, 0, j))
    wspec = lambda off: pl.BlockSpec((d, tn), lambda i, j: (0, j + off))
    cw = lambda off: pl.BlockSpec((width, tn), lambda i, j: (0, j + off))
    cb = lambda off: pl.BlockSpec((1, tn), lambda i, j: (0, j + off))
    return pl.pallas_call(
        functools.partial(_ffn_up_kernel, n_seq=s_t),
        grid=(ni, nj),
        in_specs=[pl.BlockSpec((tm, d), lambda i, j: (i, 0)),
                  wspec(0), wspec(nj), st8(0), st8(nj), cw(0), cw(nj), cb(0), cb(nj)],
        out_specs=[pl.BlockSpec((tm, tn), lambda i, j: (i, j)), out8, out8],
        out_shape=[jax.ShapeDtypeStruct((m, f), BF16),
                   jax.ShapeDtypeStruct((ni * s_t, SUBLANES, f), F32),
                   jax.ShapeDtypeStruct((ni * s_t, SUBLANES, f), F32)],
        scratch_shapes=[pltpu.VMEM((nj, s_t, SUBLANES, tn), F32), pltpu.VMEM((nj, s_t, SUBLANES, tn), F32)],
        compiler_params=_cparams("arbitrary", "arbitrary"),
        name="ffn_up",
    )(h, w_up, w_up, prev8, prev8, conv_w, conv_w, conv_b, conv_b)


def _state8(state, n_rows):
    return jnp.pad(state.astype(F32), ((0, 0), (SUBLANES - n_rows, 0), (0, 0)))


def _layer(x, mods, p, past, n_seq, seq_len):
    m, d = x.shape
    d_sb = d // 2
    d_lru = d // 2
    sh_m, sc_m, g_m, sh_f, sc_f, g_f = mods
    k_past, v_past, conv_lru_prev, h_prev, conv_ffn_prev = past

    h = _norm_mod(x, p['g_pre_mix'], sc_m, sh_m)
    (q,) = _matmul(h, p['w_in'], 0, d_sb, [(BF16, True)], name="proj_q")
    k32, k16 = _matmul(h, p['w_in'], d_sb, d_sb, [(F32, False), (BF16, True)], name="proj_k")
    v32, v16 = _matmul(h, p['w_in'], 2 * d_sb, d_sb, [(F32, False), (BF16, True)], name="proj_v")
    (xg,) = _matmul(h, p['w_in'], 3 * d_sb, 2 * d_lru, [(F32, False)], name="proj_lru")

    if k_past is None:
        o_sb = _sb_prompt(q, k16, v16)
    else:
        o_sb = _sb_sample(q, k16, v16, k_past, v_past, seq_len)

    w_c = p['w_conv_lru']
    o_lru, conv8, h8 = _lru(xg, _state8(conv_lru_prev, w_c.shape[0] - 1),
                            jnp.broadcast_to(h_prev.astype(F32)[:, None, :], (n_seq, SUBLANES, d_lru)),
                            w_c, p['b_conv_lru'], p['w_gate_a'], p['b_gate_a'], p['w_gate_x'], p['b_gate_x'],
                            p['lru_lambda'], n_seq, seq_len)
    conv_lru_new = conv8[-n_seq:, SUBLANES - (w_c.shape[0] - 1):, :]
    h_last = h8[-n_seq:, SUBLANES - 1, :]

    mix_in = _group_norm_concat(o_sb, o_lru, p['g_grp_attn'], p['g_grp_lru'])
    (mixed,) = _matmul(mix_in, p['w_out'], 0, d, [(F32, False)], name="proj_out")
    x1, h2 = _post_mix(x, mixed, p['g_post_mix'], g_m, p['g_pre_ffn'], sc_f, sh_f)

    w_f = p['w_conv_ffn']
    act, last_g, last_v = _ffn_up(h2, p['w_up'], _state8(conv_ffn_prev, w_f.shape[0] - 1),
                                  w_f, p['b_conv_ffn'], n_seq, seq_len)
    keep = SUBLANES - (w_f.shape[0] - 1)
    conv_ffn_new = jnp.concatenate([last_g[-n_seq:, keep:, :], last_v[-n_seq:, keep:, :]], axis=-1)
    (y,) = _matmul(act, p['w_down'], 0, d, [(F32, False)], tm_pref=TM_DOWN, tn_pref=TN_DOWN, name="ffn_down")
    out = _post_ffn(x1, y, p['g_post_ffn'], g_f)
    return out, (k32, v32, conv_lru_new, h_last, conv_ffn_new)


def kernel(x_prompt, x_sample, c_prompt, c_sample, cache_k, cache_v, state_conv_lru, state_lru, state_conv_ffn, w_ada, b_ada, g_pre_mix, w_in, w_conv_lru, b_conv_lru, w_gate_a, b_gate_a, w_gate_x, b_gate_x, lru_lambda, g_grp_attn, g_grp_lru, w_out, g_post_mix, g_pre_ffn, w_up, w_conv_ffn, b_conv_ffn, w_down, g_post_ffn):
    depth = w_ada.shape[0]
    bp, tp, d = x_prompt.shape
    bs, ts, _ = x_sample.shape
    assert bp == 1, "the prompt kernels treat the prompt as one sequence"
    d_sb = d // 2
    n_heads = d_sb // HEAD_DIM
    n_mod = w_ada.shape[2] // d

    xp = x_prompt.reshape(bp * tp, d)
    xs = x_sample.reshape(bs * ts, d)
    n_c = bp + bs
    n_c_pad = -(-n_c // SUBLANES) * SUBLANES
    c_all = jnp.concatenate([c_prompt, c_sample, jnp.zeros((n_c_pad - n_c, d), F32)], axis=0)

    st_p = [[], [], [], [], []]
    st_s = [[], [], [], [], []]
    for l in range(depth):
        row = lambda a: a[l][None, :]
        p = {
            'g_pre_mix': row(g_pre_mix), 'w_in': w_in[l].astype(BF16),
            'w_conv_lru': w_conv_lru[l], 'b_conv_lru': row(b_conv_lru),
            'w_gate_a': w_gate_a[l].astype(BF16), 'b_gate_a': row(b_gate_a),
            'w_gate_x': w_gate_x[l].astype(BF16), 'b_gate_x': row(b_gate_x),
            'lru_lambda': row(lru_lambda), 'g_grp_attn': row(g_grp_attn), 'g_grp_lru': row(g_grp_lru),
            'w_out': w_out[l].astype(BF16), 'g_post_mix': row(g_post_mix), 'g_pre_ffn': row(g_pre_ffn),
            'w_up': w_up[l].astype(BF16), 'w_conv_ffn': w_conv_ffn[l], 'b_conv_ffn': row(b_conv_ffn),
            'w_down': w_down[l].astype(BF16), 'g_post_ffn': row(g_post_ffn),
        }
        mod = _ada(c_all, w_ada[l], row(b_ada))
        mods_p = [mod[0:1, i * d:(i + 1) * d] for i in range(n_mod)]
        mods_s = [jnp.repeat(mod[bp:bp + bs, i * d:(i + 1) * d], ts, axis=0) for i in range(n_mod)]

        d_lru = d // 2
        f2 = w_up.shape[2]
        prompt_past = (None, None, jnp.zeros((bp, w_conv_lru.shape[1] - 1, d_lru), F32),
                       jnp.zeros((bp, d_lru), F32), jnp.zeros((bp, w_conv_ffn.shape[1] - 1, f2), F32))
        xp, new_p = _layer(xp, mods_p, p, prompt_past, bp, tp)
        past_len = cache_k.shape[2]
        sample_past = (cache_k[l].reshape(bs, past_len * n_heads, HEAD_DIM),
                       cache_v[l].reshape(bs, past_len * n_heads, HEAD_DIM),
                       state_conv_lru[l], state_lru[l], state_conv_ffn[l])
        xs, new_s = _layer(xs, mods_s, p, sample_past, bs, ts)
        for j in range(5):
            st_p[j].append(new_p[j])
            st_s[j].append(new_s[j])

    def pack(st, b, t):
        k = jnp.stack(st[0], axis=0).reshape(depth, b, t, n_heads, HEAD_DIM)
        v = jnp.stack(st[1], axis=0).reshape(depth, b, t, n_heads, HEAD_DIM)
        return k, v, jnp.stack(st[2], axis=0), jnp.stack(st[3], axis=0), jnp.stack(st[4], axis=0)

    k_p, v_p, cl_p, h_p, cf_p = pack(st_p, bp, tp)
    k_s, v_s, cl_s, h_s, cf_s = pack(st_s, bs, ts)
    return (xp.reshape(bp, tp, d), xs.reshape(bs, ts, d), k_p, v_p, cl_p, h_p, cf_p, k_s, v_s, cl_s, h_s, cf_s)
```

```python
import functools
import math

import jax
import jax.numpy as jnp
from jax import lax
from jax.experimental import pallas as pl
from jax.experimental.pallas import tpu as pltpu

F32 = jnp.float32
BF16 = jnp.bfloat16

EPS = 1e-6
LRU_C = 8.0
HEAD_DIM = 128
LANES = 128
SUBLANES = 8
MXU_DEPTH = 256
VMEM_LIMIT = 56 * 1024 * 1024
LOG_WEIGHT_ZERO = -110.0

TM_ROW = 256
TM_MM = 1024
TN_MM = 1024
TN_ADA = 512
TM_FFN = 1024
TN_FFN = 256
PIECE_FFN = 64
KCHUNKS_FFN = 16
TM_DOWN = 512
TN_DOWN = 512
TT_LRU = 1024
PC_LRU = 256
BQ_ATT = 256
TK_SAMPLE = 512


def _pick(n, pref):
    if n <= pref:
        return n
    t = pref
    while n % t:
        t //= 2
    assert t >= SUBLANES, (n, pref)
    return t


def _cparams(*sem):
    return pltpu.CompilerParams(dimension_semantics=sem, vmem_limit_bytes=VMEM_LIMIT)


def _gelu_tanh(x):
    return 0.5 * x * (1.0 + jnp.tanh(math.sqrt(2.0 / math.pi) * (x + 0.044715 * (x * x * x))))


def _rms(x, g):
    ms = jnp.mean(x * x, axis=-1, keepdims=True)
    return x * lax.rsqrt(ms + EPS) * g


def _row_tile(m, *mods):
    per_row = any(mod.shape[0] != 1 for mod in mods)
    return _pick(m, TM_ROW // 2 if per_row else TM_ROW)


def _mod_spec(mod, tm, d):
    if mod.shape[0] == 1:
        return pl.BlockSpec((1, d), lambda i: (0, 0))
    return pl.BlockSpec((tm, d), lambda i: (i, 0))


def _ada_kernel(c_ref, w_ref, b_ref, o_ref):
    c = c_ref[...]
    a = (c * jax.nn.sigmoid(c)).astype(BF16)
    o_ref[...] = jnp.dot(a, w_ref[...].astype(BF16), preferred_element_type=F32) + b_ref[...]


def _ada(c, w, b):
    bsz, d = c.shape
    n = w.shape[1]
    tn = _pick(n, TN_ADA)
    return pl.pallas_call(
        _ada_kernel,
        grid=(n // tn,),
        in_specs=[pl.BlockSpec((bsz, d), lambda j: (0, 0)),
                  pl.BlockSpec((d, tn), lambda j: (0, j)),
                  pl.BlockSpec((1, tn), lambda j: (0, j))],
        out_specs=pl.BlockSpec((bsz, tn), lambda j: (0, j)),
        out_shape=jax.ShapeDtypeStruct((bsz, n), F32),
        compiler_params=_cparams("parallel"),
        name="ada_proj",
    )(c, w, b)


def _norm_mod_kernel(x_ref, g_ref, sc_ref, sh_ref, o_ref):
    y = _rms(x_ref[...], g_ref[...])
    o_ref[...] = (y * (1.0 + sc_ref[...]) + sh_ref[...]).astype(o_ref.dtype)


def _norm_mod(x, g, sc, sh):
    m, d = x.shape
    tm = _row_tile(m, sc, sh)
    return pl.pallas_call(
        _norm_mod_kernel,
        grid=(m // tm,),
        in_specs=[pl.BlockSpec((tm, d), lambda i: (i, 0)),
                  pl.BlockSpec((1, d), lambda i: (0, 0)),
                  _mod_spec(sc, tm, d), _mod_spec(sh, tm, d)],
        out_specs=pl.BlockSpec((tm, d), lambda i: (i, 0)),
        out_shape=jax.ShapeDtypeStruct((m, d), BF16),
        compiler_params=_cparams("parallel"),
        name="norm_mod",
    )(x, g, sc, sh)


def _group_norm_kernel(a_ref, b_ref, ga_ref, gb_ref, o_ref):
    a = jnp.concatenate([a_ref[h] for h in range(a_ref.shape[0])], axis=1).astype(F32)
    da = a.shape[1]
    o_ref[:, :da] = _rms(a, ga_ref[...]).astype(o_ref.dtype)
    o_ref[:, da:] = _rms(b_ref[...].astype(F32), gb_ref[...]).astype(o_ref.dtype)


def _group_norm_concat(a, b, ga, gb):
    h, m, dh = a.shape
    da = h * dh
    db = b.shape[1]
    tm = _pick(m, TM_ROW)
    return pl.pallas_call(
        _group_norm_kernel,
        grid=(m // tm,),
        in_specs=[pl.BlockSpec((h, tm, dh), lambda i: (0, i, 0)),
                  pl.BlockSpec((tm, db), lambda i: (i, 0)),
                  pl.BlockSpec((1, da), lambda i: (0, 0)),
                  pl.BlockSpec((1, db), lambda i: (0, 0))],
        out_specs=pl.BlockSpec((tm, da + db), lambda i: (i, 0)),
        out_shape=jax.ShapeDtypeStruct((m, da + db), BF16),
        compiler_params=_cparams("parallel"),
        name="group_norm_concat",
    )(a, b, ga, gb)


def _post_mix_kernel(x_ref, y_ref, gpost_ref, gate_ref, gpre_ref, sc_ref, sh_ref, x1_ref, h_ref):
    x1 = x_ref[...] + gate_ref[...] * _rms(y_ref[...].astype(F32), gpost_ref[...])
    x1_ref[...] = x1
    h = _rms(x1, gpre_ref[...])
    h_ref[...] = (h * (1.0 + sc_ref[...]) + sh_ref[...]).astype(h_ref.dtype)


def _post_mix(x, y, gpost, gate, gpre, sc, sh):
    m, d = x.shape
    tm = _row_tile(m, gate, sc, sh)
    row = pl.BlockSpec((tm, d), lambda i: (i, 0))
    vec = pl.BlockSpec((1, d), lambda i: (0, 0))
    return pl.pallas_call(
        _post_mix_kernel,
        grid=(m // tm,),
        in_specs=[row, row, vec, _mod_spec(gate, tm, d), vec, _mod_spec(sc, tm, d), _mod_spec(sh, tm, d)],
        out_specs=[row, row],
        out_shape=[jax.ShapeDtypeStruct((m, d), F32), jax.ShapeDtypeStruct((m, d), BF16)],
        compiler_params=_cparams("parallel"),
        name="post_mix",
    )(x, y, gpost, gate, gpre, sc, sh)


def _post_ffn_kernel(x_ref, y_ref, gpost_ref, gate_ref, o_ref):
    o_ref[...] = x_ref[...] + gate_ref[...] * _rms(y_ref[...].astype(F32), gpost_ref[...])


def _post_ffn(x, y, gpost, gate):
    m, d = x.shape
    tm = _row_tile(m, gate)
    row = pl.BlockSpec((tm, d), lambda i: (i, 0))
    vec = pl.BlockSpec((1, d), lambda i: (0, 0))
    return pl.pallas_call(
        _post_ffn_kernel,
        grid=(m // tm,),
        in_specs=[row, row, vec, _mod_spec(gate, tm, d)],
        out_specs=row,
        out_shape=jax.ShapeDtypeStruct((m, d), F32),
        compiler_params=_cparams("parallel"),
        name="post_ffn",
    )(x, y, gpost, gate)


def _matmul_kernel(a_ref, b_ref, *o_refs):
    acc = jnp.dot(a_ref[...], b_ref[...], preferred_element_type=F32)
    for o_ref in o_refs:
        if len(o_ref.shape) == 3:
            for h in range(o_ref.shape[0]):
                o_ref[h] = acc[:, h * LANES:(h + 1) * LANES].astype(o_ref.dtype)
        else:
            o_ref[...] = acc.astype(o_ref.dtype)


def _matmul(a, b, col0, n, outs, tm_pref=None, tn_pref=None, name="matmul"):
    m, k = a.shape
    tm = _pick(m, TM_MM if tm_pref is None else tm_pref)
    tn = _pick(n, TN_MM if tn_pref is None else tn_pref)
    assert col0 % tn == 0 and tn % LANES == 0
    off = col0 // tn
    hpt = tn // LANES
    out_specs, out_shape = [], []
    for dt, head_major in outs:
        if head_major:
            out_specs.append(pl.BlockSpec((hpt, tm, LANES), lambda i, j: (j, i, 0)))
            out_shape.append(jax.ShapeDtypeStruct((n // LANES, m, LANES), dt))
        else:
            out_specs.append(pl.BlockSpec((tm, tn), lambda i, j: (i, j)))
            out_shape.append(jax.ShapeDtypeStruct((m, n), dt))
    return pl.pallas_call(
        _matmul_kernel,
        grid=(m // tm, n // tn),
        in_specs=[pl.BlockSpec((tm, k), lambda i, j: (i, 0)),
                  pl.BlockSpec((k, tn), lambda i, j: (0, j + off))],
        out_specs=out_specs,
        out_shape=out_shape,
        compiler_params=_cparams("parallel", "arbitrary"),
        name=name,
    )(a, b)


def _cumsum_ones(cw):
    j = lax.broadcasted_iota(jnp.int32, (2 * cw, 2 * cw), 0)
    s = lax.broadcasted_iota(jnp.int32, (2 * cw, 2 * cw), 1)
    jj = jnp.where(j >= cw, j - cw, j)
    return jnp.where((s >= cw) | (jj > s), 1.0, 0.0).astype(BF16)


def _sb_tile(q, k, v, c, uu, mask_offset):
    return _sb_tiles([(q, k, v, c)], uu, mask_offset)[0]


def _sb_tiles(items, uu, mask_offset):
    scored = [_sb_scores(q, k, uu.shape[0] // 2, mask_offset) for q, k, _, _ in items]
    xs = [x for _, _, x in scored]
    r = jnp.dot(xs[0] if len(xs) == 1 else jnp.concatenate(xs, axis=0), uu, preferred_element_type=F32)
    outs, row = [], 0
    for (_, _, v, c), (log_betas, vis, x) in zip(items, scored):
        outs.append(_sb_weights(log_betas, vis, r[row:row + x.shape[0]], c, v))
        row += x.shape[0]
    return outs


def _sb_scores(q, k, cw, mask_offset):
    bq, dh = q.shape
    nc = k.shape[0] // cw
    z = lax.dot_general(q, k, (((1,), (1,)), ((), ())), preferred_element_type=F32) * (dh ** -0.5)
    log_betas, vis, xs = [], [], []
    for ci in range(nc):
        zc = z[:, ci * cw:(ci + 1) * cw]
        log_beta = jnp.minimum(zc, 0.0) - jnp.log(1.0 + jnp.exp(-jnp.abs(zc)))
        log_keep = log_beta - zc
        if mask_offset is not None and (ci + 1) * cw > mask_offset:
            t_idx = lax.broadcasted_iota(jnp.int32, (bq, cw), 0)
            s_idx = lax.broadcasted_iota(jnp.int32, (bq, cw), 1)
            vis_c = s_idx < t_idx + (mask_offset - ci * cw)
            log_keep = jnp.where(vis_c, log_keep, 0.0)
        else:
            vis_c = None
        hi = log_keep.astype(BF16)
        lo = (log_keep - hi.astype(F32)).astype(BF16)
        log_betas.append(log_beta)
        vis.append(vis_c)
        xs.append(jnp.concatenate([hi, lo], axis=1))
    return log_betas, vis, xs[0] if nc == 1 else jnp.concatenate(xs, axis=0)


def _sb_weights(log_betas, vis, r, c, v):
    nc = len(log_betas)
    bq, cw = log_betas[0].shape
    ws = [None] * nc
    for ci in reversed(range(nc)):
        rc = r[ci * bq:(ci + 1) * bq]
        w = jnp.exp(log_betas[ci] + rc[:, :cw] + c)
        if vis[ci] is not None:
            w = jnp.where(vis[ci], w, 0.0)
        ws[ci] = w.astype(BF16)
        c = c + rc[:, cw:]
    w = ws[0] if nc == 1 else jnp.concatenate(ws, axis=1)
    return jnp.dot(w, v, preferred_element_type=F32), c


def _sb_prompt_kernel(q_ref, k_ref, v_ref, o_ref, acc_scr, c_scr, *, bq, n_par):
    n_blocks = q_ref.shape[0] // bq
    n_groups = (n_blocks - 1) // n_par
    uu = _cumsum_ones(LANES)
    c0 = jnp.zeros((bq, LANES), F32)

    acc, _ = _sb_tile(q_ref[0:bq, :], k_ref[0:bq, :], v_ref[0:bq, :], c0, uu, 0)
    o_ref[0:bq, :] = acc.astype(o_ref.dtype)

    def group(g, carry):
        items, q0s = [], []
        for u in range(n_par):
            q0 = pl.multiple_of((1 + g + u * n_groups) * bq, bq)
            k0 = pl.multiple_of(q0 - bq, bq)
            items.append((q_ref[pl.ds(q0, bq), :], k_ref[pl.ds(k0, 2 * bq), :], v_ref[pl.ds(k0, 2 * bq), :], c0))
            q0s.append(q0)
        res = _sb_tiles(items, uu, bq)
        for q0, (acc, _) in zip(q0s, res):
            o_ref[pl.ds(q0, bq), :] = acc.astype(o_ref.dtype)
        c_max = res[0][1]
        for _, c in res[1:]:
            c_max = jnp.maximum(c_max, c)

        @pl.when(jnp.max(c_max) > LOG_WEIGHT_ZERO)
        def _():
            for u, (acc, c) in enumerate(res):
                acc_scr[u] = acc
                c_scr[u] = c

            def older_keys(u, carry):
                qb = 1 + g + u * n_groups
                q0 = pl.multiple_of(qb * bq, bq)
                k0 = pl.multiple_of(q0 - bq, bq)
                q = q_ref[pl.ds(q0, bq), :]
                n_wide = (qb - 1) // 2

                def more(st):
                    j, _, c = st
                    return jnp.logical_and(j < n_wide, jnp.max(c) > LOG_WEIGHT_ZERO)

                def wide_tile(st):
                    j, acc, c = st
                    ks = pl.multiple_of(k0 - 2 * bq * (j + 1), bq)
                    da, c = _sb_tile(q, k_ref[pl.ds(ks, 2 * bq), :], v_ref[pl.ds(ks, 2 * bq), :], c, uu, None)
                    return j + 1, acc + da, c

                j, acc, c = lax.while_loop(more, wide_tile, (jnp.int32(0), acc_scr[u], c_scr[u]))
                need_last = jnp.logical_and(jnp.logical_and((qb - 1) % 2 == 1, j == n_wide),
                                            jnp.max(c) > LOG_WEIGHT_ZERO)

                def last_tile(acc, c):
                    da, _ = _sb_tile(q, k_ref[0:bq, :], v_ref[0:bq, :], c, uu, None)
                    return acc + da

                acc = lax.cond(need_last, last_tile, lambda acc, c: acc, acc, c)
                o_ref[pl.ds(q0, bq), :] = acc.astype(o_ref.dtype)
                return carry

            lax.fori_loop(0, n_par, older_keys, 0)

        return carry

    lax.fori_loop(0, n_groups, group, 0)


def _sb_prompt(q, k, v):
    h, t, dh = q.shape
    bq = _pick(t, BQ_ATT)
    n_rest = t // bq - 1
    n_par = next(u for u in (7, 4, 3, 2, 1) if n_rest % u == 0)
    spec = pl.BlockSpec((None, t, dh), lambda i: (i, 0, 0))
    return pl.pallas_call(
        functools.partial(_sb_prompt_kernel, bq=bq, n_par=n_par),
        grid=(h,),
        in_specs=[spec, spec, spec],
        out_specs=spec,
        out_shape=jax.ShapeDtypeStruct((h, t, dh), BF16),
        scratch_shapes=[pltpu.VMEM((n_par, bq, dh), F32), pltpu.VMEM((n_par, bq, LANES), F32)],
        compiler_params=_cparams("parallel"),
        name="sb_prompt",
    )(q, k, v)


def _sb_sample_kernel(q_ref, kn_ref, vn_ref, kp_ref, vp_ref, o_ref, acc_scr, c_scr, *, tk):
    n_heads, tq, dh = q_ref.shape
    kt = pl.program_id(1)
    uu = _cumsum_ones(LANES)

    @pl.when(kt == 0)
    def _():
        pad = jnp.zeros((LANES - tq, dh), BF16)
        c0 = jnp.zeros((tq, LANES), F32)
        items = [(q_ref[h], jnp.concatenate([kn_ref[h], pad], axis=0),
                  jnp.concatenate([vn_ref[h], pad], axis=0), c0) for h in range(n_heads)]
        for h, (acc, c) in enumerate(_sb_tiles(items, uu, 0)):
            acc_scr[h] = acc
            c_scr[h] = c

    items = [(q_ref[h], kp_ref[0, pl.ds(h, tk, stride=n_heads), :].astype(BF16),
              vp_ref[0, pl.ds(h, tk, stride=n_heads), :].astype(BF16), c_scr[h]) for h in range(n_heads)]
    for h, (da, c) in enumerate(_sb_tiles(items, uu, None)):
        acc_scr[h] = acc_scr[h] + da
        c_scr[h] = c

    @pl.when(kt == pl.num_programs(1) - 1)
    def _():
        for h in range(n_heads):
            o_ref[h] = acc_scr[h].astype(o_ref.dtype)


def _sb_sample(q, kn, vn, kp, vp, tq):
    h, m, dh = q.shape
    bsz = m // tq
    p = kp.shape[1] // h
    tk = _pick(p, TK_SAMPLE)
    n_kt = p // tk
    assert tq <= LANES and tk % LANES == 0
    new = pl.BlockSpec((h, tq, dh), lambda b, kt: (0, b, 0))
    past = pl.BlockSpec((1, tk * h, dh), lambda b, kt: (b, n_kt - 1 - kt, 0))
    return pl.pallas_call(
        functools.partial(_sb_sample_kernel, tk=tk),
        grid=(bsz, n_kt),
        in_specs=[new, new, new, past, past],
        out_specs=new,
        out_shape=jax.ShapeDtypeStruct((h, m, dh), BF16),
        scratch_shapes=[pltpu.VMEM((h, tq, dh), F32), pltpu.VMEM((h, tq, LANES), F32)],
        compiler_params=_cparams("parallel", "arbitrary"),
        name="sb_sample",
    )(q, kn, vn, kp, vp)


def _shift_rows(x, prev8, s, row8):
    rolled = pltpu.roll(x, s, 0)
    first = jnp.where(row8 < s, pltpu.roll(prev8, s, 0), rolled[:SUBLANES])
    if x.shape[0] == SUBLANES:
        return first
    return jnp.concatenate([first, rolled[SUBLANES:]], axis=0)


def _causal_conv(x, prev8, w, b):
    width = w.shape[0]
    row8 = lax.broadcasted_iota(jnp.int32, (SUBLANES, x.shape[1]), 0)
    y = b + w[width - 1:width] * x
    for j in range(width - 1):
        y = y + w[j:j + 1] * _shift_rows(x, prev8, width - 1 - j, row8)
    return y


def _linear_scan(a, u, h_in):
    n = a.shape[0]
    sub = jnp.bitwise_and(lax.broadcasted_iota(jnp.int32, a.shape, 0), SUBLANES - 1)
    d = 1
    while d < SUBLANES:
        keep = sub >= d
        a_sh = jnp.where(keep, pltpu.roll(a, d, 0), 1.0)
        u_sh = jnp.where(keep, pltpu.roll(u, d, 0), 0.0)
        u = u + a * u_sh
        a = a * a_sh
        d *= 2
    hs = []
    for g in range(n // SUBLANES):
        rows = slice(g * SUBLANES, (g + 1) * SUBLANES)
        hg = a[rows] * h_in + u[rows]
        hs.append(hg)
        h_in = hg[SUBLANES - 1:SUBLANES, :]
    return hs[0] if len(hs) == 1 else jnp.concatenate(hs, axis=0)


def _lru_kernel(xl_ref, gl_ref, prev_ref, h0_ref, cw_ref, cb_ref, wa_ref, ba_ref, wx_ref, bx_ref, lam_ref,
                o_ref, conv8_ref, h8_ref, prev_scr, h_scr, *, n_seq, piece):
    t = pl.program_id(1)

    @pl.when(t == 0)
    def _():
        prev_scr[...] = prev_ref[...]
        h_scr[...] = h0_ref[...]

    lam = lam_ref[...]
    softplus_neg_lam = jnp.maximum(-lam, 0.0) + jnp.log1p(jnp.exp(-jnp.abs(lam)))
    wa = wa_ref[0]
    wx = wx_ref[0]
    seq_rows = xl_ref.shape[0] // n_seq
    for s in range(n_seq):
        prev8 = prev_scr[s]
        h8 = h_scr[s]
        for pc in range(seq_rows // piece):
            rows = slice(s * seq_rows + pc * piece, s * seq_rows + (pc + 1) * piece)
            x = xl_ref[rows, :]
            xc = _causal_conv(x, prev8, cw_ref[...], cb_ref[...])
            xcb = xc.astype(BF16)
            r = jax.nn.sigmoid(jnp.dot(xcb, wa, preferred_element_type=F32) + ba_ref[...])
            i = jax.nn.sigmoid(jnp.dot(xcb, wx, preferred_element_type=F32) + bx_ref[...])
            log_a = -LRU_C * r * softplus_neg_lam
            a = jnp.exp(log_a)
            u = jnp.sqrt(-jnp.tanh(log_a) * (a * a + 1.0)) * (i * xc)
            hs = _linear_scan(a, u, h8[SUBLANES - 1:SUBLANES, :])
            o_ref[rows, :] = (hs * _gelu_tanh(gl_ref[rows, :])).astype(o_ref.dtype)
            prev8 = x[piece - SUBLANES:, :]
            h8 = hs[piece - SUBLANES:, :]
        prev_scr[s] = prev8
        h_scr[s] = h8
        conv8_ref[s] = prev8
        h8_ref[s] = h8


def _lru(xg, prev8, h08, conv_w, conv_b, w_a, b_a, w_x, b_x, lam, n_seq, seq_len):
    m, c2 = xg.shape
    c = c2 // 2
    nb = c // LANES
    assert w_a.shape == (nb, LANES, LANES) and seq_len % SUBLANES == 0 and seq_len >= conv_w.shape[0] - 1
    if n_seq == 1:
        tt = _pick(seq_len, TT_LRU)
        s_t, piece = 1, _pick(tt, PC_LRU)
    else:
        tt = m
        s_t, piece = n_seq, seq_len
    nt = m // tt
    width = conv_w.shape[0]
    vec = pl.BlockSpec((1, LANES), lambda n, t: (0, n))
    st8 = pl.BlockSpec((s_t, SUBLANES, LANES), lambda n, t: (0, 0, n))
    out8 = pl.BlockSpec((s_t, SUBLANES, LANES), lambda n, t: (t, 0, n))
    gate_w = pl.BlockSpec((1, LANES, LANES), lambda n, t: (n, 0, 0))
    return pl.pallas_call(
        functools.partial(_lru_kernel, n_seq=s_t, piece=piece),
        grid=(nb, nt),
        in_specs=[pl.BlockSpec((tt, LANES), lambda n, t: (t, n)),
                  pl.BlockSpec((tt, LANES), lambda n, t: (t, n + nb)),
                  st8, st8,
                  pl.BlockSpec((width, LANES), lambda n, t: (0, n)), vec,
                  gate_w, vec, gate_w, vec, vec],
        out_specs=[pl.BlockSpec((tt, LANES), lambda n, t: (t, n)), out8, out8],
        out_shape=[jax.ShapeDtypeStruct((m, c), BF16),
                   jax.ShapeDtypeStruct((nt * s_t, SUBLANES, c), F32),
                   jax.ShapeDtypeStruct((nt * s_t, SUBLANES, c), F32)],
        scratch_shapes=[pltpu.VMEM((s_t, SUBLANES, LANES), F32), pltpu.VMEM((s_t, SUBLANES, LANES), F32)],
        compiler_params=_cparams("parallel", "arbitrary"),
        name="rg_lru",
    )(xg, xg, prev8, h08, conv_w, conv_b, w_a, b_a, w_x, b_x, lam)


def _ffn_up_kernel(h_ref, wg_ref, wv_ref, pg_ref, pv_ref, cwg_ref, cwv_ref, cbg_ref, cbv_ref,
                   act_ref, lastg_ref, lastv_ref, raw_g, raw_v, carry_g, carry_v, *, n_seq, nj):
    s = pl.program_id(0)
    tile_e = jnp.maximum(s - 1, 0)
    i_e = lax.div(tile_e, nj)
    j_e = lax.rem(tile_e, nj)

    @pl.when(s == 0)
    def _():
        raw_g[1] = jnp.zeros(raw_g.shape[1:], F32)
        raw_v[1] = jnp.zeros(raw_v.shape[1:], F32)

    @pl.when(i_e == 0)
    def _():
        carry_g[j_e] = pg_ref[...]
        carry_v[j_e] = pv_ref[...]

    valid = s > 0
    tm = h_ref.shape[0]
    seq_rows = tm // n_seq
    piece = min(seq_rows, PIECE_FFN)

    def step(cur, prv):
        pieces = [(sq, p0) for sq in range(n_seq) for p0 in range(sq * seq_rows, (sq + 1) * seq_rows, piece)]
        n_k = max(1, min(KCHUNKS_FFN, len(pieces), h_ref.shape[1] // MXU_DEPTH))
        kc_size = h_ref.shape[1] // n_k
        prev_g = [carry_g[j_e, sq] for sq in range(n_seq)]
        prev_v = [carry_v[j_e, sq] for sq in range(n_seq)]
        old_g, old_v = list(prev_g), list(prev_v)
        acc_g = acc_v = None
        done = 0
        for kc in range(n_k):
            ks = slice(kc * kc_size, (kc + 1) * kc_size)
            h = h_ref[:, ks]
            part_g = jnp.dot(h, wg_ref[ks, :], preferred_element_type=F32)
            part_v = jnp.dot(h, wv_ref[ks, :], preferred_element_type=F32)
            acc_g = part_g if acc_g is None else acc_g + part_g
            acc_v = part_v if acc_v is None else acc_v + part_v
            upto = len(pieces) * (kc + 1) // n_k
            for sq, p0 in pieces[done:upto]:
                xg = raw_g[prv, p0:p0 + piece, :]
                xv = raw_v[prv, p0:p0 + piece, :]
                yg = _causal_conv(xg, prev_g[sq], cwg_ref[...], cbg_ref[...])
                yv = _causal_conv(xv, prev_v[sq], cwv_ref[...], cbv_ref[...])
                act_ref[p0:p0 + piece, :] = (_gelu_tanh(yg) * yv).astype(act_ref.dtype)
                prev_g[sq] = xg[piece - SUBLANES:, :]
                prev_v[sq] = xv[piece - SUBLANES:, :]
            done = upto
        raw_g[cur] = acc_g
        raw_v[cur] = acc_v
        for sq in range(n_seq):
            carry_g[j_e, sq] = jnp.where(valid, prev_g[sq], old_g[sq])
            carry_v[j_e, sq] = jnp.where(valid, prev_v[sq], old_v[sq])
            lastg_ref[sq] = prev_g[sq]
            lastv_ref[sq] = prev_v[sq]

    slot = lax.rem(s, 2)
    pl.when(slot == 0)(lambda: step(0, 1))
    pl.when(slot == 1)(lambda: step(1, 0))


def _ffn_up(h, w_up, prev8, conv_w, conv_b, n_seq, seq_len):
    m, d = h.shape
    f = w_up.shape[1] // 2
    tn = _pick(f, TN_FFN)
    assert seq_len % SUBLANES == 0 and seq_len >= conv_w.shape[0] - 1
    if n_seq == 1:
        tm = _pick(seq_len, TM_FFN)
        s_t = 1
    else:
        tm = m
        s_t = n_seq
    ni, nj = m // tm, f // tn
    n_tiles = ni * nj
    width = conv_w.shape[0]
    i_d = lambda s: jnp.minimum(s, n_tiles - 1) // nj
    j_d = lambda s: jnp.minimum(s, n_tiles - 1) % nj
    i_e = lambda s: jnp.maximum(s - 1, 0) // nj
    j_e = lambda s: jnp.maximum(s - 1, 0) % nj
    st8 = lambda off: pl.BlockSpec((s_t, SUBLANES, tn), lambda s: (0, 0, j_e(s) + off))
    out8 = pl.BlockSpec((s_t, SUBLANES, tn), lambda s: (i_e(s), 0, j_e(s)))
    wspec = lambda off: pl.BlockSpec((d, tn), lambda s: (0, j_d(s) + off))
    cw = lambda off: pl.BlockSpec((width, tn), lambda s: (0, j_e(s) + off))
    cb = lambda off: pl.BlockSpec((1, tn), lambda s: (0, j_e(s) + off))
    return pl.pallas_call(
        functools.partial(_ffn_up_kernel, n_seq=s_t, nj=nj),
        grid=(n_tiles + 1,),
        in_specs=[pl.BlockSpec((tm, d), lambda s: (i_d(s), 0)),
                  wspec(0), wspec(nj), st8(0), st8(nj), cw(0), cw(nj), cb(0), cb(nj)],
        out_specs=[pl.BlockSpec((tm, tn), lambda s: (i_e(s), j_e(s))), out8, out8],
        out_shape=[jax.ShapeDtypeStruct((m, f), BF16),
                   jax.ShapeDtypeStruct((ni * s_t, SUBLANES, f), F32),
                   jax.ShapeDtypeStruct((ni * s_t, SUBLANES, f), F32)],
        scratch_shapes=[pltpu.VMEM((2, tm, tn), F32), pltpu.VMEM((2, tm, tn), F32),
                        pltpu.VMEM((nj, s_t, SUBLANES, tn), F32), pltpu.VMEM((nj, s_t, SUBLANES, tn), F32)],
        compiler_params=_cparams("arbitrary"),
        name="ffn_up",
    )(h, w_up, w_up, prev8, prev8, conv_w, conv_w, conv_b, conv_b)


def _state8(state, n_rows):
    return jnp.pad(state.astype(F32), ((0, 0), (SUBLANES - n_rows, 0), (0, 0)))


def _layer(x, mods, p, past, n_seq, seq_len):
    m, d = x.shape
    d_sb = d // 2
    d_lru = d // 2
    sh_m, sc_m, g_m, sh_f, sc_f, g_f = mods
    k_past, v_past, conv_lru_prev, h_prev, conv_ffn_prev = past

    h = _norm_mod(x, p['g_pre_mix'], sc_m, sh_m)
    (q,) = _matmul(h, p['w_in'], 0, d_sb, [(BF16, True)], name="proj_q")
    k32, k16 = _matmul(h, p['w_in'], d_sb, d_sb, [(F32, False), (BF16, True)], name="proj_k")
    v32, v16 = _matmul(h, p['w_in'], 2 * d_sb, d_sb, [(F32, False), (BF16, True)], name="proj_v")
    (xg,) = _matmul(h, p['w_in'], 3 * d_sb, 2 * d_lru, [(F32, False)], name="proj_lru")

    if k_past is None:
        o_sb = _sb_prompt(q, k16, v16)
    else:
        o_sb = _sb_sample(q, k16, v16, k_past, v_past, seq_len)

    w_c = p['w_conv_lru']
    o_lru, conv8, h8 = _lru(xg, _state8(conv_lru_prev, w_c.shape[0] - 1),
                            jnp.broadcast_to(h_prev.astype(F32)[:, None, :], (n_seq, SUBLANES, d_lru)),
                            w_c, p['b_conv_lru'], p['w_gate_a'], p['b_gate_a'], p['w_gate_x'], p['b_gate_x'],
                            p['lru_lambda'], n_seq, seq_len)
    conv_lru_new = conv8[-n_seq:, SUBLANES - (w_c.shape[0] - 1):, :]
    h_last = h8[-n_seq:, SUBLANES - 1, :]

    mix_in = _group_norm_concat(o_sb, o_lru, p['g_grp_attn'], p['g_grp_lru'])
    (mixed,) = _matmul(mix_in, p['w_out'], 0, d, [(BF16, False)], name="proj_out")
    x1, h2 = _post_mix(x, mixed, p['g_post_mix'], g_m, p['g_pre_ffn'], sc_f, sh_f)

    w_f = p['w_conv_ffn']
    act, last_g, last_v = _ffn_up(h2, p['w_up'], _state8(conv_ffn_prev, w_f.shape[0] - 1),
                                  w_f, p['b_conv_ffn'], n_seq, seq_len)
    keep = SUBLANES - (w_f.shape[0] - 1)
    conv_ffn_new = jnp.concatenate([last_g[-n_seq:, keep:, :], last_v[-n_seq:, keep:, :]], axis=-1)
    (y,) = _matmul(act, p['w_down'], 0, d, [(BF16, False)], tm_pref=TM_DOWN, tn_pref=TN_DOWN, name="ffn_down")
    out = _post_ffn(x1, y, p['g_post_ffn'], g_f)
    return out, (k32, v32, conv_lru_new, h_last, conv_ffn_new)


def kernel(x_prompt, x_sample, c_prompt, c_sample, cache_k, cache_v, state_conv_lru, state_lru, state_conv_ffn, w_ada, b_ada, g_pre_mix, w_in, w_conv_lru, b_conv_lru, w_gate_a, b_gate_a, w_gate_x, b_gate_x, lru_lambda, g_grp_attn, g_grp_lru, w_out, g_post_mix, g_pre_ffn, w_up, w_conv_ffn, b_conv_ffn, w_down, g_post_ffn):
    depth = w_ada.shape[0]
    bp, tp, d = x_prompt.shape
    bs, ts, _ = x_sample.shape
    assert bp == 1, "the prompt kernels treat the prompt as one sequence"
    d_sb = d // 2
    n_heads = d_sb // HEAD_DIM
    n_mod = w_ada.shape[2] // d

    xp = x_prompt.reshape(bp * tp, d)
    xs = x_sample.reshape(bs * ts, d)
    n_c = bp + bs
    n_c_pad = -(-n_c // SUBLANES) * SUBLANES
    c_all = jnp.concatenate([c_prompt, c_sample, jnp.zeros((n_c_pad - n_c, d), F32)], axis=0)

    st_p = [[], [], [], [], []]
    st_s = [[], [], [], [], []]
    for l in range(depth):
        row = lambda a: a[l][None, :]
        p = {
            'g_pre_mix': row(g_pre_mix), 'w_in': w_in[l].astype(BF16),
            'w_conv_lru': w_conv_lru[l], 'b_conv_lru': row(b_conv_lru),
            'w_gate_a': w_gate_a[l].astype(BF16), 'b_gate_a': row(b_gate_a),
            'w_gate_x': w_gate_x[l].astype(BF16), 'b_gate_x': row(b_gate_x),
            'lru_lambda': row(lru_lambda), 'g_grp_attn': row(g_grp_attn), 'g_grp_lru': row(g_grp_lru),
            'w_out': w_out[l].astype(BF16), 'g_post_mix': row(g_post_mix), 'g_pre_ffn': row(g_pre_ffn),
            'w_up': w_up[l].astype(BF16), 'w_conv_ffn': w_conv_ffn[l], 'b_conv_ffn': row(b_conv_ffn),
            'w_down': w_down[l].astype(BF16), 'g_post_ffn': row(g_post_ffn),
        }
        mod = _ada(c_all, w_ada[l], row(b_ada))
        mods_p = [mod[0:1, i * d:(i + 1) * d] for i in range(n_mod)]
        mods_s = [jnp.repeat(mod[bp:bp + bs, i * d:(i + 1) * d], ts, axis=0) for i in range(n_mod)]

        d_lru = d // 2
        f2 = w_up.shape[2]
        prompt_past = (None, None, jnp.zeros((bp, w_conv_lru.shape[1] - 1, d_lru), F32),
                       jnp.zeros((bp, d_lru), F32), jnp.zeros((bp, w_conv_ffn.shape[1] - 1, f2), F32))
        xp, new_p = _layer(xp, mods_p, p, prompt_past, bp, tp)
        past_len = cache_k.shape[2]
        sample_past = (cache_k[l].reshape(bs, past_len * n_heads, HEAD_DIM),
                       cache_v[l].reshape(bs, past_len * n_heads, HEAD_DIM),
                       state_conv_lru[l], state_lru[l], state_conv_ffn[l])
        xs, new_s = _layer(xs, mods_s, p, sample_past, bs, ts)
        for j in range(5):
            st_p[j].append(new_p[j])
            st_s[j].append(new_s[j])

    def pack(st, b, t):
        k = jnp.stack(st[0], axis=0).reshape(depth, b, t, n_heads, HEAD_DIM)
        v = jnp.stack(st[1], axis=0).reshape(depth, b, t, n_heads, HEAD_DIM)
        return k, v, jnp.stack(st[2], axis=0), jnp.stack(st[3], axis=0), jnp.stack(st[4], axis=0)

    k_p, v_p, cl_p, h_p, cf_p = pack(st_p, bp, tp)
    k_s, v_s, cl_s, h_s, cf_s = pack(st_s, bs, ts)
    return (xp.reshape(bp, tp, d), xs.reshape(bs, ts, d), k_p, v_p, cl_p, h_p, cf_p, k_s, v_s, cl_s, h_s, cf_s)
```

```python
import functools
import math

import jax
import jax.numpy as jnp
from jax import lax
from jax.experimental import pallas as pl
from jax.experimental.pallas import tpu as pltpu

F32 = jnp.float32
BF16 = jnp.bfloat16

EPS = 1e-6
LRU_C = 8.0
HEAD_DIM = 128
LANES = 128
SUBLANES = 8
MXU_DEPTH = 256
BF16_ROWS = 16
VMEM_LIMIT = 56 * 1024 * 1024
LOG_WEIGHT_ZERO = -110.0

TM_ROW = 256
TM_MM = 1024
TN_MM = 1024
TN_ADA = 512
TM_FFN = 1024
TN_FFN = 256
PIECE_FFN = 64
KCHUNKS_FFN = 16
TM_DOWN = 512
TN_DOWN = 512
TT_LRU = 1024
PC_LRU = 256
BQ_ATT = 256
TK_SAMPLE = 512


def _pick(n, pref):
    if n <= pref:
        return n
    t = pref
    while n % t:
        t //= 2
    assert t >= SUBLANES, (n, pref)
    return t


def _cparams(*sem):
    return pltpu.CompilerParams(dimension_semantics=sem, vmem_limit_bytes=VMEM_LIMIT)


def _gelu_tanh(x):
    return 0.5 * x * (1.0 + jnp.tanh(math.sqrt(2.0 / math.pi) * (x + 0.044715 * (x * x * x))))


def _rms(x, g):
    ms = jnp.mean(x * x, axis=-1, keepdims=True)
    return x * lax.rsqrt(ms + EPS) * g


def _row_tile(m, *mods):
    per_row = any(mod.shape[0] != 1 for mod in mods)
    return _pick(m, TM_ROW // 2 if per_row else TM_ROW)


def _mod_spec(mod, tm, d):
    if mod.shape[0] == 1:
        return pl.BlockSpec((1, d), lambda i: (0, 0))
    return pl.BlockSpec((tm, d), lambda i: (i, 0))


def _ada_kernel(c_ref, w_ref, b_ref, o_ref):
    c = c_ref[...]
    a = (c * jax.nn.sigmoid(c)).astype(BF16)
    o_ref[...] = jnp.dot(a, w_ref[...].astype(BF16), preferred_element_type=F32) + b_ref[...]


def _ada(c, w, b):
    bsz, d = c.shape
    n = w.shape[1]
    tn = _pick(n, TN_ADA)
    return pl.pallas_call(
        _ada_kernel,
        grid=(n // tn,),
        in_specs=[pl.BlockSpec((bsz, d), lambda j: (0, 0)),
                  pl.BlockSpec((d, tn), lambda j: (0, j)),
                  pl.BlockSpec((1, tn), lambda j: (0, j))],
        out_specs=pl.BlockSpec((bsz, tn), lambda j: (0, j)),
        out_shape=jax.ShapeDtypeStruct((bsz, n), F32),
        compiler_params=_cparams("parallel"),
        name="ada_proj",
    )(c, w, b)


def _norm_mod_kernel(x_ref, g_ref, sc_ref, sh_ref, o_ref):
    y = _rms(x_ref[...], g_ref[...])
    o_ref[...] = (y * (1.0 + sc_ref[...]) + sh_ref[...]).astype(o_ref.dtype)


def _norm_mod(x, g, sc, sh):
    m, d = x.shape
    tm = _row_tile(m, sc, sh)
    return pl.pallas_call(
        _norm_mod_kernel,
        grid=(m // tm,),
        in_specs=[pl.BlockSpec((tm, d), lambda i: (i, 0)),
                  pl.BlockSpec((1, d), lambda i: (0, 0)),
                  _mod_spec(sc, tm, d), _mod_spec(sh, tm, d)],
        out_specs=pl.BlockSpec((tm, d), lambda i: (i, 0)),
        out_shape=jax.ShapeDtypeStruct((m, d), BF16),
        compiler_params=_cparams("parallel"),
        name="norm_mod",
    )(x, g, sc, sh)


def _group_norm_kernel(a_ref, b_ref, ga_ref, gb_ref, o_ref):
    a = jnp.concatenate([a_ref[h] for h in range(a_ref.shape[0])], axis=1).astype(F32)
    da = a.shape[1]
    o_ref[:, :da] = _rms(a, ga_ref[...]).astype(o_ref.dtype)
    o_ref[:, da:] = _rms(b_ref[...].astype(F32), gb_ref[...]).astype(o_ref.dtype)


def _group_norm_concat(a, b, ga, gb):
    h, m, dh = a.shape
    da = h * dh
    db = b.shape[1]
    tm = _pick(m, TM_ROW)
    return pl.pallas_call(
        _group_norm_kernel,
        grid=(m // tm,),
        in_specs=[pl.BlockSpec((h, tm, dh), lambda i: (0, i, 0)),
                  pl.BlockSpec((tm, db), lambda i: (i, 0)),
                  pl.BlockSpec((1, da), lambda i: (0, 0)),
                  pl.BlockSpec((1, db), lambda i: (0, 0))],
        out_specs=pl.BlockSpec((tm, da + db), lambda i: (i, 0)),
        out_shape=jax.ShapeDtypeStruct((m, da + db), BF16),
        compiler_params=_cparams("parallel"),
        name="group_norm_concat",
    )(a, b, ga, gb)


def _post_mix_kernel(x_ref, y_ref, gpost_ref, gate_ref, gpre_ref, sc_ref, sh_ref, x1_ref, h_ref):
    x1 = x_ref[...] + gate_ref[...] * _rms(y_ref[...].astype(F32), gpost_ref[...])
    x1_ref[...] = x1
    h = _rms(x1, gpre_ref[...])
    h_ref[...] = (h * (1.0 + sc_ref[...]) + sh_ref[...]).astype(h_ref.dtype)


def _post_mix(x, y, gpost, gate, gpre, sc, sh):
    m, d = x.shape
    tm = _row_tile(m, gate, sc, sh)
    row = pl.BlockSpec((tm, d), lambda i: (i, 0))
    vec = pl.BlockSpec((1, d), lambda i: (0, 0))
    return pl.pallas_call(
        _post_mix_kernel,
        grid=(m // tm,),
        in_specs=[row, row, vec, _mod_spec(gate, tm, d), vec, _mod_spec(sc, tm, d), _mod_spec(sh, tm, d)],
        out_specs=[row, row],
        out_shape=[jax.ShapeDtypeStruct((m, d), F32), jax.ShapeDtypeStruct((m, d), BF16)],
        compiler_params=_cparams("parallel"),
        name="post_mix",
    )(x, y, gpost, gate, gpre, sc, sh)


def _post_ffn_kernel(x_ref, y_ref, gpost_ref, gate_ref, o_ref):
    o_ref[...] = x_ref[...] + gate_ref[...] * _rms(y_ref[...].astype(F32), gpost_ref[...])


def _post_ffn(x, y, gpost, gate):
    m, d = x.shape
    tm = _row_tile(m, gate)
    row = pl.BlockSpec((tm, d), lambda i: (i, 0))
    vec = pl.BlockSpec((1, d), lambda i: (0, 0))
    return pl.pallas_call(
        _post_ffn_kernel,
        grid=(m // tm,),
        in_specs=[row, row, vec, _mod_spec(gate, tm, d)],
        out_specs=row,
        out_shape=jax.ShapeDtypeStruct((m, d), F32),
        compiler_params=_cparams("parallel"),
        name="post_ffn",
    )(x, y, gpost, gate)


def _matmul_kernel(a_ref, b_ref, *o_refs):
    acc = jnp.dot(a_ref[...], b_ref[...], preferred_element_type=F32)
    for o_ref in o_refs:
        if len(o_ref.shape) == 3:
            for h in range(o_ref.shape[0]):
                o_ref[h] = acc[:, h * LANES:(h + 1) * LANES].astype(o_ref.dtype)
        else:
            o_ref[...] = acc.astype(o_ref.dtype)


def _matmul(a, b, col0, n, outs, tm_pref=None, tn_pref=None, name="matmul"):
    m, k = a.shape
    tm = _pick(m, TM_MM if tm_pref is None else tm_pref)
    tn = _pick(n, TN_MM if tn_pref is None else tn_pref)
    assert col0 % tn == 0 and tn % LANES == 0
    off = col0 // tn
    hpt = tn // LANES
    out_specs, out_shape = [], []
    for dt, head_major in outs:
        if head_major:
            out_specs.append(pl.BlockSpec((hpt, tm, LANES), lambda i, j: (j, i, 0)))
            out_shape.append(jax.ShapeDtypeStruct((n // LANES, m, LANES), dt))
        else:
            out_specs.append(pl.BlockSpec((tm, tn), lambda i, j: (i, j)))
            out_shape.append(jax.ShapeDtypeStruct((m, n), dt))
    return pl.pallas_call(
        _matmul_kernel,
        grid=(m // tm, n // tn),
        in_specs=[pl.BlockSpec((tm, k), lambda i, j: (i, 0)),
                  pl.BlockSpec((k, tn), lambda i, j: (0, j + off))],
        out_specs=out_specs,
        out_shape=out_shape,
        compiler_params=_cparams("parallel", "arbitrary"),
        name=name,
    )(a, b)


def _cumsum_ones(cw):
    j = lax.broadcasted_iota(jnp.int32, (2 * cw, 2 * cw), 0)
    s = lax.broadcasted_iota(jnp.int32, (2 * cw, 2 * cw), 1)
    jj = jnp.where(j >= cw, j - cw, j)
    return jnp.where((s >= cw) | (jj > s), 1.0, 0.0).astype(BF16)


def _sb_tile(q, k, v, c, uu, mask_offset):
    return _sb_tiles([(q, k, v, c)], uu, mask_offset)[0]


def _sb_tiles(items, uu, mask_offset):
    scored = [_sb_scores(q, k, uu.shape[0] // 2, mask_offset) for q, k, _, _ in items]
    xs = [x for _, x in scored]
    r = jnp.dot(xs[0] if len(xs) == 1 else jnp.concatenate(xs, axis=0), uu, preferred_element_type=F32)
    outs, row = [], 0
    for (_, _, v, c), (chunks, x) in zip(items, scored):
        outs.append(_sb_weights(chunks, r[row:row + x.shape[0]], c, v))
        row += x.shape[0]
    return outs


def _sb_scores(q, k, cw, mask_offset):
    bq, dh = q.shape
    nc = k.shape[0] // cw
    z = lax.dot_general(q, k, (((1,), (1,)), ((), ())), preferred_element_type=F32) * (dh ** -0.5)
    chunks, xs = [], []
    for ci in range(nc):
        masked = mask_offset is not None and (ci + 1) * cw > mask_offset
        row0 = 0
        if masked:
            row0 = min(max(ci * cw - mask_offset + 1, 0), bq - BF16_ROWS) // BF16_ROWS * BF16_ROWS
        zc = z[row0:, ci * cw:(ci + 1) * cw]
        log_beta = jnp.minimum(zc, 0.0) - jnp.log(1.0 + jnp.exp(-jnp.abs(zc)))
        log_keep = log_beta - zc
        vis = None
        if masked:
            t_idx = lax.broadcasted_iota(jnp.int32, zc.shape, 0)
            s_idx = lax.broadcasted_iota(jnp.int32, zc.shape, 1)
            vis = s_idx < t_idx + (row0 + mask_offset - ci * cw)
            log_keep = jnp.where(vis, log_keep, 0.0)
        hi = log_keep.astype(BF16)
        lo = (log_keep - hi.astype(F32)).astype(BF16)
        chunks.append((log_beta, vis, row0))
        xs.append(jnp.concatenate([hi, lo], axis=1))
    return chunks, xs[0] if nc == 1 else jnp.concatenate(xs, axis=0)


def _sb_weights(chunks, r, c, v):
    cw = chunks[0][0].shape[1]
    ends, row = [], 0
    for log_beta, _, _ in chunks:
        row += log_beta.shape[0]
        ends.append(row)
    ws = [None] * len(chunks)
    for ci in reversed(range(len(chunks))):
        log_beta, vis, row0 = chunks[ci]
        rc = r[ends[ci] - log_beta.shape[0]:ends[ci]]
        w = jnp.exp(log_beta + rc[:, :cw] + c[row0:])
        if vis is not None:
            w = jnp.where(vis, w, 0.0)
        w = w.astype(BF16)
        c_new = c[row0:] + rc[:, cw:]
        if row0:
            w = jnp.concatenate([jnp.zeros((row0, cw), BF16), w], axis=0)
            c_new = jnp.concatenate([c[:row0], c_new], axis=0)
        ws[ci] = w
        c = c_new
    w = ws[0] if len(ws) == 1 else jnp.concatenate(ws, axis=1)
    return jnp.dot(w, v, preferred_element_type=F32), c


def _sb_prompt_kernel(q_ref, k_ref, v_ref, o_ref, acc_scr, c_scr, *, bq, n_par):
    n_blocks = q_ref.shape[0] // bq
    n_groups = (n_blocks - 1) // n_par
    uu = _cumsum_ones(LANES)
    c0 = jnp.zeros((bq, LANES), F32)

    acc, _ = _sb_tile(q_ref[0:bq, :], k_ref[0:bq, :], v_ref[0:bq, :], c0, uu, 0)
    o_ref[0:bq, :] = acc.astype(o_ref.dtype)

    def group(g, carry):
        items, q0s = [], []
        for u in range(n_par):
            q0 = pl.multiple_of((1 + g + u * n_groups) * bq, bq)
            k0 = pl.multiple_of(q0 - bq, bq)
            items.append((q_ref[pl.ds(q0, bq), :], k_ref[pl.ds(k0, 2 * bq), :], v_ref[pl.ds(k0, 2 * bq), :], c0))
            q0s.append(q0)
        res = _sb_tiles(items, uu, bq)
        for q0, (acc, _) in zip(q0s, res):
            o_ref[pl.ds(q0, bq), :] = acc.astype(o_ref.dtype)
        c_max = res[0][1]
        for _, c in res[1:]:
            c_max = jnp.maximum(c_max, c)

        @pl.when(jnp.max(c_max) > LOG_WEIGHT_ZERO)
        def _():
            for u, (acc, c) in enumerate(res):
                acc_scr[u] = acc
                c_scr[u] = c

            def older_keys(u, carry):
                qb = 1 + g + u * n_groups
                q0 = pl.multiple_of(qb * bq, bq)
                k0 = pl.multiple_of(q0 - bq, bq)
                q = q_ref[pl.ds(q0, bq), :]
                n_wide = (qb - 1) // 2

                def more(st):
                    j, _, c = st
                    return jnp.logical_and(j < n_wide, jnp.max(c) > LOG_WEIGHT_ZERO)

                def wide_tile(st):
                    j, acc, c = st
                    ks = pl.multiple_of(k0 - 2 * bq * (j + 1), bq)
                    da, c = _sb_tile(q, k_ref[pl.ds(ks, 2 * bq), :], v_ref[pl.ds(ks, 2 * bq), :], c, uu, None)
                    return j + 1, acc + da, c

                j, acc, c = lax.while_loop(more, wide_tile, (jnp.int32(0), acc_scr[u], c_scr[u]))
                need_last = jnp.logical_and(jnp.logical_and((qb - 1) % 2 == 1, j == n_wide),
                                            jnp.max(c) > LOG_WEIGHT_ZERO)

                def last_tile(acc, c):
                    da, _ = _sb_tile(q, k_ref[0:bq, :], v_ref[0:bq, :], c, uu, None)
                    return acc + da

                acc = lax.cond(need_last, last_tile, lambda acc, c: acc, acc, c)
                o_ref[pl.ds(q0, bq), :] = acc.astype(o_ref.dtype)
                return carry

            lax.fori_loop(0, n_par, older_keys, 0)

        return carry

    lax.fori_loop(0, n_groups, group, 0)


def _sb_prompt(q, k, v):
    h, t, dh = q.shape
    bq = _pick(t, BQ_ATT)
    n_rest = t // bq - 1
    n_par = next(u for u in (7, 4, 3, 2, 1) if n_rest % u == 0)
    spec = pl.BlockSpec((None, t, dh), lambda i: (i, 0, 0))
    return pl.pallas_call(
        functools.partial(_sb_prompt_kernel, bq=bq, n_par=n_par),
        grid=(h,),
        in_specs=[spec, spec, spec],
        out_specs=spec,
        out_shape=jax.ShapeDtypeStruct((h, t, dh), BF16),
        scratch_shapes=[pltpu.VMEM((n_par, bq, dh), F32), pltpu.VMEM((n_par, bq, LANES), F32)],
        compiler_params=_cparams("parallel"),
        name="sb_prompt",
    )(q, k, v)


def _sb_sample_kernel(q_ref, kn_ref, vn_ref, kp_ref, vp_ref, o_ref, acc_scr, c_scr, *, tk):
    n_heads, tq, dh = q_ref.shape
    kt = pl.program_id(1)
    uu = _cumsum_ones(LANES)

    @pl.when(kt == 0)
    def _():
        pad = jnp.zeros((LANES - tq, dh), BF16)
        c0 = jnp.zeros((tq, LANES), F32)
        items = [(q_ref[h], jnp.concatenate([kn_ref[h], pad], axis=0),
                  jnp.concatenate([vn_ref[h], pad], axis=0), c0) for h in range(n_heads)]
        for h, (acc, c) in enumerate(_sb_tiles(items, uu, 0)):
            acc_scr[h] = acc
            c_scr[h] = c

    items = [(q_ref[h], kp_ref[0, pl.ds(h, tk, stride=n_heads), :].astype(BF16),
              vp_ref[0, pl.ds(h, tk, stride=n_heads), :].astype(BF16), c_scr[h]) for h in range(n_heads)]
    for h, (da, c) in enumerate(_sb_tiles(items, uu, None)):
        acc_scr[h] = acc_scr[h] + da
        c_scr[h] = c

    @pl.when(kt == pl.num_programs(1) - 1)
    def _():
        for h in range(n_heads):
            o_ref[h] = acc_scr[h].astype(o_ref.dtype)


def _sb_sample(q, kn, vn, kp, vp, tq):
    h, m, dh = q.shape
    bsz = m // tq
    p = kp.shape[1] // h
    tk = _pick(p, TK_SAMPLE)
    n_kt = p // tk
    assert tq <= LANES and tk % LANES == 0
    new = pl.BlockSpec((h, tq, dh), lambda b, kt: (0, b, 0))
    past = pl.BlockSpec((1, tk * h, dh), lambda b, kt: (b, n_kt - 1 - kt, 0))
    return pl.pallas_call(
        functools.partial(_sb_sample_kernel, tk=tk),
        grid=(bsz, n_kt),
        in_specs=[new, new, new, past, past],
        out_specs=new,
        out_shape=jax.ShapeDtypeStruct((h, m, dh), BF16),
        scratch_shapes=[pltpu.VMEM((h, tq, dh), F32), pltpu.VMEM((h, tq, LANES), F32)],
        compiler_params=_cparams("parallel", "arbitrary"),
        name="sb_sample",
    )(q, kn, vn, kp, vp)


def _shift_rows(x, prev8, s):
    n, c = x.shape
    groups = n // SUBLANES
    rot = pltpu.roll(x.reshape(groups, SUBLANES, c), s, 1)
    before = pltpu.roll(prev8, s, 0).reshape(1, SUBLANES, c)
    if groups > 1:
        before = jnp.concatenate([before, rot[:groups - 1]], axis=0)
    sub = lax.broadcasted_iota(jnp.int32, rot.shape, 1)
    return jnp.where(sub < s, before, rot).reshape(n, c)


def _causal_conv(x, prev8, w, b):
    width = w.shape[0]
    y = b + w[width - 1:width] * x
    for j in range(width - 1):
        y = y + w[j:j + 1] * _shift_rows(x, prev8, width - 1 - j)
    return y


def _linear_scan(a, u, h_in):
    n, c = a.shape
    a = a.reshape(n // SUBLANES, SUBLANES, c)
    u = u.reshape(n // SUBLANES, SUBLANES, c)
    sub = lax.broadcasted_iota(jnp.int32, a.shape, 1)
    d = 1
    while d < SUBLANES:
        keep = sub >= d
        a_sh = jnp.where(keep, pltpu.roll(a, d, 1), 1.0)
        u_sh = jnp.where(keep, pltpu.roll(u, d, 1), 0.0)
        u = u + a * u_sh
        a = a * a_sh
        d *= 2
    a = a.reshape(n, c)
    u = u.reshape(n, c)
    hs = []
    for g in range(n // SUBLANES):
        rows = slice(g * SUBLANES, (g + 1) * SUBLANES)
        hg = a[rows] * h_in + u[rows]
        hs.append(hg)
        h_in = hg[SUBLANES - 1:SUBLANES, :]
    return hs[0] if len(hs) == 1 else jnp.concatenate(hs, axis=0)


def _lru_kernel(xl_ref, gl_ref, prev_ref, h0_ref, cw_ref, cb_ref, wa_ref, ba_ref, wx_ref, bx_ref, lam_ref,
                o_ref, conv8_ref, h8_ref, prev_scr, h_scr, *, n_seq, piece):
    t = pl.program_id(1)

    @pl.when(t == 0)
    def _():
        prev_scr[...] = prev_ref[...]
        h_scr[...] = h0_ref[...]

    lam = lam_ref[...]
    softplus_neg_lam = jnp.maximum(-lam, 0.0) + jnp.log1p(jnp.exp(-jnp.abs(lam)))
    wa = wa_ref[0]
    wx = wx_ref[0]
    seq_rows = xl_ref.shape[0] // n_seq
    for s in range(n_seq):
        prev8 = prev_scr[s]
        h8 = h_scr[s]
        for pc in range(seq_rows // piece):
            rows = slice(s * seq_rows + pc * piece, s * seq_rows + (pc + 1) * piece)
            x = xl_ref[rows, :]
            xc = _causal_conv(x, prev8, cw_ref[...], cb_ref[...])
            xcb = xc.astype(BF16)
            r = jax.nn.sigmoid(jnp.dot(xcb, wa, preferred_element_type=F32) + ba_ref[...])
            i = jax.nn.sigmoid(jnp.dot(xcb, wx, preferred_element_type=F32) + bx_ref[...])
            log_a = -LRU_C * r * softplus_neg_lam
            a = jnp.exp(log_a)
            u = jnp.sqrt(-jnp.tanh(log_a) * (a * a + 1.0)) * (i * xc)
            hs = _linear_scan(a, u, h8[SUBLANES - 1:SUBLANES, :])
            o_ref[rows, :] = (hs * _gelu_tanh(gl_ref[rows, :])).astype(o_ref.dtype)
            prev8 = x[piece - SUBLANES:, :]
            h8 = hs[piece - SUBLANES:, :]
        prev_scr[s] = prev8
        h_scr[s] = h8
        conv8_ref[s] = prev8
        h8_ref[s] = h8


def _lru(xg, prev8, h08, conv_w, conv_b, w_a, b_a, w_x, b_x, lam, n_seq, seq_len):
    m, c2 = xg.shape
    c = c2 // 2
    nb = c // LANES
    assert w_a.shape == (nb, LANES, LANES) and seq_len % SUBLANES == 0 and seq_len >= conv_w.shape[0] - 1
    if n_seq == 1:
        tt = _pick(seq_len, TT_LRU)
        s_t, piece = 1, _pick(tt, PC_LRU)
    else:
        tt = m
        s_t, piece = n_seq, seq_len
    nt = m // tt
    width = conv_w.shape[0]
    vec = pl.BlockSpec((1, LANES), lambda n, t: (0, n))
    st8 = pl.BlockSpec((s_t, SUBLANES, LANES), lambda n, t: (0, 0, n))
    out8 = pl.BlockSpec((s_t, SUBLANES, LANES), lambda n, t: (t, 0, n))
    gate_w = pl.BlockSpec((1, LANES, LANES), lambda n, t: (n, 0, 0))
    return pl.pallas_call(
        functools.partial(_lru_kernel, n_seq=s_t, piece=piece),
        grid=(nb, nt),
        in_specs=[pl.BlockSpec((tt, LANES), lambda n, t: (t, n)),
                  pl.BlockSpec((tt, LANES), lambda n, t: (t, n + nb)),
                  st8, st8,
                  pl.BlockSpec((width, LANES), lambda n, t: (0, n)), vec,
                  gate_w, vec, gate_w, vec, vec],
        out_specs=[pl.BlockSpec((tt, LANES), lambda n, t: (t, n)), out8, out8],
        out_shape=[jax.ShapeDtypeStruct((m, c), BF16),
                   jax.ShapeDtypeStruct((nt * s_t, SUBLANES, c), F32),
                   jax.ShapeDtypeStruct((nt * s_t, SUBLANES, c), F32)],
        scratch_shapes=[pltpu.VMEM((s_t, SUBLANES, LANES), F32), pltpu.VMEM((s_t, SUBLANES, LANES), F32)],
        compiler_params=_cparams("parallel", "arbitrary"),
        name="rg_lru",
    )(xg, xg, prev8, h08, conv_w, conv_b, w_a, b_a, w_x, b_x, lam)


def _ffn_up_kernel(h_ref, wg_ref, wv_ref, pg_ref, pv_ref, cwg_ref, cwv_ref, cbg_ref, cbv_ref,
                   act_ref, lastg_ref, lastv_ref, raw_g, raw_v, carry_g, carry_v, *, n_seq, nj):
    s = pl.program_id(0)
    tile_e = jnp.maximum(s - 1, 0)
    i_e = lax.div(tile_e, nj)
    j_e = lax.rem(tile_e, nj)

    @pl.when(s == 0)
    def _():
        raw_g[1] = jnp.zeros(raw_g.shape[1:], F32)
        raw_v[1] = jnp.zeros(raw_v.shape[1:], F32)

    @pl.when(i_e == 0)
    def _():
        carry_g[j_e] = pg_ref[...]
        carry_v[j_e] = pv_ref[...]

    valid = s > 0
    tm = h_ref.shape[0]
    seq_rows = tm // n_seq
    piece = min(seq_rows, PIECE_FFN)

    def step(cur, prv):
        pieces = [(sq, p0) for sq in range(n_seq) for p0 in range(sq * seq_rows, (sq + 1) * seq_rows, piece)]
        n_k = max(1, min(KCHUNKS_FFN, len(pieces), h_ref.shape[1] // MXU_DEPTH))
        kc_size = h_ref.shape[1] // n_k
        prev_g = [carry_g[j_e, sq] for sq in range(n_seq)]
        prev_v = [carry_v[j_e, sq] for sq in range(n_seq)]
        old_g, old_v = list(prev_g), list(prev_v)
        acc_g = acc_v = None
        done = 0
        for kc in range(n_k):
            ks = slice(kc * kc_size, (kc + 1) * kc_size)
            h = h_ref[:, ks]
            part_g = jnp.dot(h, wg_ref[ks, :].astype(BF16), preferred_element_type=F32)
            part_v = jnp.dot(h, wv_ref[ks, :].astype(BF16), preferred_element_type=F32)
            acc_g = part_g if acc_g is None else acc_g + part_g
            acc_v = part_v if acc_v is None else acc_v + part_v
            upto = len(pieces) * (kc + 1) // n_k
            for sq, p0 in pieces[done:upto]:
                xg = raw_g[prv, p0:p0 + piece, :]
                xv = raw_v[prv, p0:p0 + piece, :]
                yg = _causal_conv(xg, prev_g[sq], cwg_ref[...], cbg_ref[...])
                yv = _causal_conv(xv, prev_v[sq], cwv_ref[...], cbv_ref[...])
                act_ref[p0:p0 + piece, :] = (_gelu_tanh(yg) * yv).astype(act_ref.dtype)
                prev_g[sq] = xg[piece - SUBLANES:, :]
                prev_v[sq] = xv[piece - SUBLANES:, :]
            done = upto
        raw_g[cur] = acc_g
        raw_v[cur] = acc_v
        for sq in range(n_seq):
            carry_g[j_e, sq] = jnp.where(valid, prev_g[sq], old_g[sq])
            carry_v[j_e, sq] = jnp.where(valid, prev_v[sq], old_v[sq])
            lastg_ref[sq] = prev_g[sq]
            lastv_ref[sq] = prev_v[sq]

    slot = lax.rem(s, 2)
    pl.when(slot == 0)(lambda: step(0, 1))
    pl.when(slot == 1)(lambda: step(1, 0))


def _ffn_up(h, w_up, prev8, conv_w, conv_b, n_seq, seq_len):
    m, d = h.shape
    f = w_up.shape[1] // 2
    tn = _pick(f, TN_FFN)
    assert seq_len % SUBLANES == 0 and seq_len >= conv_w.shape[0] - 1
    if n_seq == 1:
        tm = _pick(seq_len, TM_FFN)
        s_t = 1
    else:
        tm = m
        s_t = n_seq
    ni, nj = m // tm, f // tn
    n_tiles = ni * nj
    width = conv_w.shape[0]
    i_d = lambda s: jnp.minimum(s, n_tiles - 1) // nj
    j_d = lambda s: jnp.minimum(s, n_tiles - 1) % nj
    i_e = lambda s: jnp.maximum(s - 1, 0) // nj
    j_e = lambda s: jnp.maximum(s - 1, 0) % nj
    st8 = lambda off: pl.BlockSpec((s_t, SUBLANES, tn), lambda s: (0, 0, j_e(s) + off))
    out8 = pl.BlockSpec((s_t, SUBLANES, tn), lambda s: (i_e(s), 0, j_e(s)))
    wspec = lambda off: pl.BlockSpec((d, tn), lambda s: (0, j_d(s) + off))
    cw = lambda off: pl.BlockSpec((width, tn), lambda s: (0, j_e(s) + off))
    cb = lambda off: pl.BlockSpec((1, tn), lambda s: (0, j_e(s) + off))
    return pl.pallas_call(
        functools.partial(_ffn_up_kernel, n_seq=s_t, nj=nj),
        grid=(n_tiles + 1,),
        in_specs=[pl.BlockSpec((tm, d), lambda s: (i_d(s), 0)),
                  wspec(0), wspec(nj), st8(0), st8(nj), cw(0), cw(nj), cb(0), cb(nj)],
        out_specs=[pl.BlockSpec((tm, tn), lambda s: (i_e(s), j_e(s))), out8, out8],
        out_shape=[jax.ShapeDtypeStruct((m, f), BF16),
                   jax.ShapeDtypeStruct((ni * s_t, SUBLANES, f), F32),
                   jax.ShapeDtypeStruct((ni * s_t, SUBLANES, f), F32)],
        scratch_shapes=[pltpu.VMEM((2, tm, tn), F32), pltpu.VMEM((2, tm, tn), F32),
                        pltpu.VMEM((nj, s_t, SUBLANES, tn), F32), pltpu.VMEM((nj, s_t, SUBLANES, tn), F32)],
        compiler_params=_cparams("arbitrary"),
        name="ffn_up",
    )(h, w_up, w_up, prev8, prev8, conv_w, conv_w, conv_b, conv_b)


def _state8(state, n_rows):
    return jnp.pad(state.astype(F32), ((0, 0), (SUBLANES - n_rows, 0), (0, 0)))


def _layer(x, mods, p, past, n_seq, seq_len):
    m, d = x.shape
    d_sb = d // 2
    d_lru = d // 2
    sh_m, sc_m, g_m, sh_f, sc_f, g_f = mods
    k_past, v_past, conv_lru_prev, h_prev, conv_ffn_prev = past

    h = _norm_mod(x, p['g_pre_mix'], sc_m, sh_m)
    (q,) = _matmul(h, p['w_in'], 0, d_sb, [(BF16, True)], name="proj_q")
    k32, k16 = _matmul(h, p['w_in'], d_sb, d_sb, [(F32, False), (BF16, True)], name="proj_k")
    v32, v16 = _matmul(h, p['w_in'], 2 * d_sb, d_sb, [(F32, False), (BF16, True)], name="proj_v")
    (xg,) = _matmul(h, p['w_in'], 3 * d_sb, 2 * d_lru, [(F32, False)], name="proj_lru")

    if k_past is None:
        o_sb = _sb_prompt(q, k16, v16)
    else:
        o_sb = _sb_sample(q, k16, v16, k_past, v_past, seq_len)

    w_c = p['w_conv_lru']
    o_lru, conv8, h8 = _lru(xg, _state8(conv_lru_prev, w_c.shape[0] - 1),
                            jnp.broadcast_to(h_prev.astype(F32)[:, None, :], (n_seq, SUBLANES, d_lru)),
                            w_c, p['b_conv_lru'], p['w_gate_a'], p['b_gate_a'], p['w_gate_x'], p['b_gate_x'],
                            p['lru_lambda'], n_seq, seq_len)
    conv_lru_new = conv8[-n_seq:, SUBLANES - (w_c.shape[0] - 1):, :]
    h_last = h8[-n_seq:, SUBLANES - 1, :]

    mix_in = _group_norm_concat(o_sb, o_lru, p['g_grp_attn'], p['g_grp_lru'])
    (mixed,) = _matmul(mix_in, p['w_out'], 0, d, [(BF16, False)], name="proj_out")
    x1, h2 = _post_mix(x, mixed, p['g_post_mix'], g_m, p['g_pre_ffn'], sc_f, sh_f)

    w_f = p['w_conv_ffn']
    act, last_g, last_v = _ffn_up(h2, p['w_up'], _state8(conv_ffn_prev, w_f.shape[0] - 1),
                                  w_f, p['b_conv_ffn'], n_seq, seq_len)
    keep = SUBLANES - (w_f.shape[0] - 1)
    conv_ffn_new = jnp.concatenate([last_g[-n_seq:, keep:, :], last_v[-n_seq:, keep:, :]], axis=-1)
    (y,) = _matmul(act, p['w_down'], 0, d, [(BF16, False)], tm_pref=TM_DOWN, tn_pref=TN_DOWN, name="ffn_down")
    out = _post_ffn(x1, y, p['g_post_ffn'], g_f)
    return out, (k32, v32, conv_lru_new, h_last, conv_ffn_new)


def kernel(x_prompt, x_sample, c_prompt, c_sample, cache_k, cache_v, state_conv_lru, state_lru, state_conv_ffn, w_ada, b_ada, g_pre_mix, w_in, w_conv_lru, b_conv_lru, w_gate_a, b_gate_a, w_gate_x, b_gate_x, lru_lambda, g_grp_attn, g_grp_lru, w_out, g_post_mix, g_pre_ffn, w_up, w_conv_ffn, b_conv_ffn, w_down, g_post_ffn):
    depth = w_ada.shape[0]
    bp, tp, d = x_prompt.shape
    bs, ts, _ = x_sample.shape
    assert bp == 1, "the prompt kernels treat the prompt as one sequence"
    d_sb = d // 2
    n_heads = d_sb // HEAD_DIM
    n_mod = w_ada.shape[2] // d

    xp = x_prompt.reshape(bp * tp, d)
    xs = x_sample.reshape(bs * ts, d)
    n_c = bp + bs
    n_c_pad = -(-n_c // SUBLANES) * SUBLANES
    c_all = jnp.concatenate([c_prompt, c_sample, jnp.zeros((n_c_pad - n_c, d), F32)], axis=0)

    st_p = [[], [], [], [], []]
    st_s = [[], [], [], [], []]
    for l in range(depth):
        row = lambda a: a[l][None, :]
        p = {
            'g_pre_mix': row(g_pre_mix), 'w_in': w_in[l].astype(BF16),
            'w_conv_lru': w_conv_lru[l], 'b_conv_lru': row(b_conv_lru),
            'w_gate_a': w_gate_a[l].astype(BF16), 'b_gate_a': row(b_gate_a),
            'w_gate_x': w_gate_x[l].astype(BF16), 'b_gate_x': row(b_gate_x),
            'lru_lambda': row(lru_lambda), 'g_grp_attn': row(g_grp_attn), 'g_grp_lru': row(g_grp_lru),
            'w_out': w_out[l].astype(BF16), 'g_post_mix': row(g_post_mix), 'g_pre_ffn': row(g_pre_ffn),
            'w_up': w_up[l], 'w_conv_ffn': w_conv_ffn[l], 'b_conv_ffn': row(b_conv_ffn),
            'w_down': w_down[l].astype(BF16), 'g_post_ffn': row(g_post_ffn),
        }
        mod = _ada(c_all, w_ada[l], row(b_ada))
        mods_p = [mod[0:1, i * d:(i + 1) * d] for i in range(n_mod)]
        mods_s = [jnp.repeat(mod[bp:bp + bs, i * d:(i + 1) * d], ts, axis=0) for i in range(n_mod)]

        d_lru = d // 2
        f2 = w_up.shape[2]
        prompt_past = (None, None, jnp.zeros((bp, w_conv_lru.shape[1] - 1, d_lru), F32),
                       jnp.zeros((bp, d_lru), F32), jnp.zeros((bp, w_conv_ffn.shape[1] - 1, f2), F32))
        xp, new_p = _layer(xp, mods_p, p, prompt_past, bp, tp)
        past_len = cache_k.shape[2]
        sample_past = (cache_k[l].reshape(bs, past_len * n_heads, HEAD_DIM),
                       cache_v[l].reshape(bs, past_len * n_heads, HEAD_DIM),
                       state_conv_lru[l], state_lru[l], state_conv_ffn[l])
        xs, new_s = _layer(xs, mods_s, p, sample_past, bs, ts)
        for j in range(5):
            st_p[j].append(new_p[j])
            st_s[j].append(new_s[j])

    def pack(st, b, t):
        k = jnp.stack(st[0], axis=0).reshape(depth, b, t, n_heads, HEAD_DIM)
        v = jnp.stack(st[1], axis=0).reshape(depth, b, t, n_heads, HEAD_DIM)
        return k, v, jnp.stack(st[2], axis=0), jnp.stack(st[3], axis=0), jnp.stack(st[4], axis=0)

    k_p, v_p, cl_p, h_p, cf_p = pack(st_p, bp, tp)
    k_s, v_s, cl_s, h_s, cf_s = pack(st_s, bs, ts)
    return (xp.reshape(bp, tp, d), xs.reshape(bs, ts, d), k_p, v_p, cl_p, h_p, cf_p, k_s, v_s, cl_s, h_s, cf_s)
```

```python
import functools
import math

import jax
import jax.numpy as jnp
from jax import lax
from jax.experimental import pallas as pl
from jax.experimental.pallas import tpu as pltpu

F32 = jnp.float32
BF16 = jnp.bfloat16

EPS = 1e-6
LRU_C = 8.0
HEAD_DIM = 128
LANES = 128
SUBLANES = 8
MXU_DEPTH = 256
BF16_ROWS = 16
VMEM_LIMIT = 56 * 1024 * 1024
LOG_WEIGHT_ZERO = -110.0

TM_ROW = 256
TM_MM = 1024
TN_MM = 1024
TN_ADA = 512
TM_FFN = 1024
TN_FFN = 256
PIECE_FFN = 64
KCHUNKS_FFN = 16
TM_DOWN = 512
TN_DOWN = 512
KCHUNKS_DOWN = 4
TT_LRU = 1024
PC_LRU = 256
BQ_ATT = 256
TK_SAMPLE = 512


def _pick(n, pref):
    if n <= pref:
        return n
    t = pref
    while n % t:
        t //= 2
    assert t >= SUBLANES, (n, pref)
    return t


def _cparams(*sem):
    return pltpu.CompilerParams(dimension_semantics=sem, vmem_limit_bytes=VMEM_LIMIT)


def _gelu_tanh(x):
    return 0.5 * x * (1.0 + jnp.tanh(math.sqrt(2.0 / math.pi) * (x + 0.044715 * (x * x * x))))


def _rms(x, g):
    ms = jnp.mean(x * x, axis=-1, keepdims=True)
    return x * lax.rsqrt(ms + EPS) * g


def _row_tile(m, *mods):
    per_row = any(mod.shape[0] != 1 for mod in mods)
    return _pick(m, TM_ROW // 2 if per_row else TM_ROW)


def _mod_spec(mod, tm, d):
    if mod.shape[0] == 1:
        return pl.BlockSpec((1, d), lambda i: (0, 0))
    return pl.BlockSpec((tm, d), lambda i: (i, 0))


def _ada_kernel(c_ref, w_ref, b_ref, o_ref):
    c = c_ref[...]
    a = (c * jax.nn.sigmoid(c)).astype(BF16)
    o_ref[...] = jnp.dot(a, w_ref[...].astype(BF16), preferred_element_type=F32) + b_ref[...]


def _ada(c, w, b):
    bsz, d = c.shape
    n = w.shape[1]
    tn = _pick(n, TN_ADA)
    return pl.pallas_call(
        _ada_kernel,
        grid=(n // tn,),
        in_specs=[pl.BlockSpec((bsz, d), lambda j: (0, 0)),
                  pl.BlockSpec((d, tn), lambda j: (0, j)),
                  pl.BlockSpec((1, tn), lambda j: (0, j))],
        out_specs=pl.BlockSpec((bsz, tn), lambda j: (0, j)),
        out_shape=jax.ShapeDtypeStruct((bsz, n), F32),
        compiler_params=_cparams("parallel"),
        name="ada_proj",
    )(c, w, b)


def _norm_mod_kernel(x_ref, g_ref, sc_ref, sh_ref, o_ref):
    y = _rms(x_ref[...], g_ref[...])
    o_ref[...] = (y * (1.0 + sc_ref[...]) + sh_ref[...]).astype(o_ref.dtype)


def _norm_mod(x, g, sc, sh):
    m, d = x.shape
    tm = _row_tile(m, sc, sh)
    return pl.pallas_call(
        _norm_mod_kernel,
        grid=(m // tm,),
        in_specs=[pl.BlockSpec((tm, d), lambda i: (i, 0)),
                  pl.BlockSpec((1, d), lambda i: (0, 0)),
                  _mod_spec(sc, tm, d), _mod_spec(sh, tm, d)],
        out_specs=pl.BlockSpec((tm, d), lambda i: (i, 0)),
        out_shape=jax.ShapeDtypeStruct((m, d), BF16),
        compiler_params=_cparams("parallel"),
        name="norm_mod",
    )(x, g, sc, sh)


def _group_norm_kernel(a_ref, b_ref, ga_ref, gb_ref, o_ref):
    a = jnp.concatenate([a_ref[h] for h in range(a_ref.shape[0])], axis=1).astype(F32)
    da = a.shape[1]
    o_ref[:, :da] = _rms(a, ga_ref[...]).astype(o_ref.dtype)
    o_ref[:, da:] = _rms(b_ref[...].astype(F32), gb_ref[...]).astype(o_ref.dtype)


def _group_norm_concat(a, b, ga, gb):
    h, m, dh = a.shape
    da = h * dh
    db = b.shape[1]
    tm = _pick(m, TM_ROW)
    return pl.pallas_call(
        _group_norm_kernel,
        grid=(m // tm,),
        in_specs=[pl.BlockSpec((h, tm, dh), lambda i: (0, i, 0)),
                  pl.BlockSpec((tm, db), lambda i: (i, 0)),
                  pl.BlockSpec((1, da), lambda i: (0, 0)),
                  pl.BlockSpec((1, db), lambda i: (0, 0))],
        out_specs=pl.BlockSpec((tm, da + db), lambda i: (i, 0)),
        out_shape=jax.ShapeDtypeStruct((m, da + db), BF16),
        compiler_params=_cparams("parallel"),
        name="group_norm_concat",
    )(a, b, ga, gb)


def _post_mix_kernel(x_ref, y_ref, gpost_ref, gate_ref, gpre_ref, sc_ref, sh_ref, x1_ref, h_ref):
    x1 = x_ref[...] + gate_ref[...] * _rms(y_ref[...].astype(F32), gpost_ref[...])
    x1_ref[...] = x1
    h = _rms(x1, gpre_ref[...])
    h_ref[...] = (h * (1.0 + sc_ref[...]) + sh_ref[...]).astype(h_ref.dtype)


def _post_mix(x, y, gpost, gate, gpre, sc, sh):
    m, d = x.shape
    tm = _row_tile(m, gate, sc, sh)
    row = pl.BlockSpec((tm, d), lambda i: (i, 0))
    vec = pl.BlockSpec((1, d), lambda i: (0, 0))
    return pl.pallas_call(
        _post_mix_kernel,
        grid=(m // tm,),
        in_specs=[row, row, vec, _mod_spec(gate, tm, d), vec, _mod_spec(sc, tm, d), _mod_spec(sh, tm, d)],
        out_specs=[row, row],
        out_shape=[jax.ShapeDtypeStruct((m, d), F32), jax.ShapeDtypeStruct((m, d), BF16)],
        compiler_params=_cparams("parallel"),
        name="post_mix",
    )(x, y, gpost, gate, gpre, sc, sh)


def _ffn_down_kernel(a_ref, b_ref, x_ref, gpost_ref, gate_ref, o_ref, y_scr, ssq_scr, *, nj, ni):
    s = pl.program_id(0)
    row = lax.div(s, nj)
    j = lax.rem(s, nj)
    cur = lax.rem(row, 2)
    tm, tn = o_ref.shape

    @pl.when(s == 0)
    def _():
        y_scr[...] = jnp.zeros(y_scr.shape, y_scr.dtype)
        ssq_scr[...] = jnp.zeros(ssq_scr.shape, F32)

    def residual(rows):
        ms = ssq_scr[1 - cur, rows, 0:1] * (1.0 / (nj * tn))
        gate = gate_ref[rows, :] if gate_ref.shape[0] > 1 else gate_ref[...]
        y_prev = y_scr[j, rows, :].astype(F32)
        o_ref[rows, :] = x_ref[rows, :] + gate * (y_prev * lax.rsqrt(ms + EPS) * gpost_ref[...])

    k = a_ref.shape[1]
    n_k = 1
    while 2 * n_k <= min(KCHUNKS_DOWN, k // MXU_DEPTH) and tm % (2 * n_k * SUBLANES) == 0:
        n_k *= 2
    bounds = [k // MXU_DEPTH * c // n_k * MXU_DEPTH for c in range(n_k)] + [k]
    rows_per = tm // n_k

    @pl.when(row < ni)
    def _():
        y = None
        for c in range(n_k):
            ks = slice(bounds[c], bounds[c + 1])
            part = jnp.dot(a_ref[:, ks], b_ref[ks, :], preferred_element_type=F32)
            y = part if y is None else y + part
            residual(slice(c * rows_per, (c + 1) * rows_per))
        y_scr[j] = y.astype(y_scr.dtype)
        sq = jnp.broadcast_to(jnp.sum(y * y, axis=1, keepdims=True), (tm, LANES))
        ssq_scr[cur] = jnp.where(j == 0, sq, ssq_scr[cur] + sq)

    @pl.when(row == ni)
    def _():
        residual(slice(0, tm))


def _ffn_down_residual(a, b, x, gpost, gate):
    m, k = a.shape
    n = b.shape[1]
    tm = _pick(m, TM_DOWN)
    tn = _pick(n, TN_DOWN)
    ni, nj = m // tm, n // tn
    r_d = lambda s: jnp.minimum(s // nj, ni - 1)
    r_e = lambda s: jnp.maximum(s // nj - 1, 0)
    j_e = lambda s: jnp.where(s < nj, 0, s % nj)
    if gate.shape[0] == 1:
        gate_spec = pl.BlockSpec((1, tn), lambda s: (0, j_e(s)))
    else:
        gate_spec = pl.BlockSpec((tm, tn), lambda s: (r_e(s), j_e(s)))
    return pl.pallas_call(
        functools.partial(_ffn_down_kernel, nj=nj, ni=ni),
        grid=((ni + 1) * nj,),
        in_specs=[pl.BlockSpec((tm, k), lambda s: (r_d(s), 0)),
                  pl.BlockSpec((k, tn), lambda s: (0, jnp.where(s // nj < ni, s % nj, nj - 1))),
                  pl.BlockSpec((tm, tn), lambda s: (r_e(s), j_e(s))),
                  pl.BlockSpec((1, tn), lambda s: (0, j_e(s))),
                  gate_spec],
        out_specs=pl.BlockSpec((tm, tn), lambda s: (r_e(s), j_e(s))),
        out_shape=jax.ShapeDtypeStruct((m, n), F32),
        scratch_shapes=[pltpu.VMEM((nj, tm, tn), BF16), pltpu.VMEM((2, tm, LANES), F32)],
        compiler_params=_cparams("arbitrary"),
        name="ffn_down",
    )(a, b, x, gpost, gate)


def _matmul_kernel(a_ref, b_ref, *o_refs):
    acc = jnp.dot(a_ref[...], b_ref[...], preferred_element_type=F32)
    for o_ref in o_refs:
        if len(o_ref.shape) == 3:
            for h in range(o_ref.shape[0]):
                o_ref[h] = acc[:, h * LANES:(h + 1) * LANES].astype(o_ref.dtype)
        else:
            o_ref[...] = acc.astype(o_ref.dtype)


def _matmul(a, b, col0, n, outs, tm_pref=None, tn_pref=None, name="matmul"):
    m, k = a.shape
    tm = _pick(m, TM_MM if tm_pref is None else tm_pref)
    tn = _pick(n, TN_MM if tn_pref is None else tn_pref)
    assert col0 % tn == 0 and tn % LANES == 0
    off = col0 // tn
    hpt = tn // LANES
    out_specs, out_shape = [], []
    for dt, head_major in outs:
        if head_major:
            out_specs.append(pl.BlockSpec((hpt, tm, LANES), lambda i, j: (j, i, 0)))
            out_shape.append(jax.ShapeDtypeStruct((n // LANES, m, LANES), dt))
        else:
            out_specs.append(pl.BlockSpec((tm, tn), lambda i, j: (i, j)))
            out_shape.append(jax.ShapeDtypeStruct((m, n), dt))
    return pl.pallas_call(
        _matmul_kernel,
        grid=(m // tm, n // tn),
        in_specs=[pl.BlockSpec((tm, k), lambda i, j: (i, 0)),
                  pl.BlockSpec((k, tn), lambda i, j: (0, j + off))],
        out_specs=out_specs,
        out_shape=out_shape,
        compiler_params=_cparams("parallel", "arbitrary"),
        name=name,
    )(a, b)


def _cumsum_ones(cw):
    j = lax.broadcasted_iota(jnp.int32, (2 * cw, 2 * cw), 0)
    s = lax.broadcasted_iota(jnp.int32, (2 * cw, 2 * cw), 1)
    jj = jnp.where(j >= cw, j - cw, j)
    return jnp.where((s >= cw) | (jj > s), 1.0, 0.0).astype(BF16)


def _sb_tile(q, k, v, c, uu, mask_offset):
    return _sb_tiles([(q, k, v, c)], uu, mask_offset)[0]


def _sb_tiles(items, uu, mask_offset):
    scored = [_sb_scores(q, k, uu.shape[0] // 2, mask_offset) for q, k, _, _ in items]
    xs = [x for _, x in scored]
    r = jnp.dot(xs[0] if len(xs) == 1 else jnp.concatenate(xs, axis=0), uu, preferred_element_type=F32)
    outs, row = [], 0
    for (_, _, v, c), (chunks, x) in zip(items, scored):
        outs.append(_sb_weights(chunks, r[row:row + x.shape[0]], c, v))
        row += x.shape[0]
    return outs


def _sb_scores(q, k, cw, mask_offset):
    bq, dh = q.shape
    nc = k.shape[0] // cw
    z = lax.dot_general(q, k, (((1,), (1,)), ((), ())), preferred_element_type=F32) * (dh ** -0.5)
    chunks, xs = [], []
    for ci in range(nc):
        masked = mask_offset is not None and (ci + 1) * cw > mask_offset
        row0 = 0
        if masked:
            row0 = min(max(ci * cw - mask_offset + 1, 0), bq - BF16_ROWS) // BF16_ROWS * BF16_ROWS
        zc = z[row0:, ci * cw:(ci + 1) * cw]
        log_beta = jnp.minimum(zc, 0.0) - jnp.log(1.0 + jnp.exp(-jnp.abs(zc)))
        log_keep = log_beta - zc
        vis = None
        if masked:
            t_idx = lax.broadcasted_iota(jnp.int32, zc.shape, 0)
            s_idx = lax.broadcasted_iota(jnp.int32, zc.shape, 1)
            vis = s_idx < t_idx + (row0 + mask_offset - ci * cw)
            log_keep = jnp.where(vis, log_keep, 0.0)
        hi = log_keep.astype(BF16)
        lo = (log_keep - hi.astype(F32)).astype(BF16)
        chunks.append((log_beta, vis, row0))
        xs.append(jnp.concatenate([hi, lo], axis=1))
    return chunks, xs[0] if nc == 1 else jnp.concatenate(xs, axis=0)


def _sb_weights(chunks, r, c, v):
    cw = chunks[0][0].shape[1]
    ends, row = [], 0
    for log_beta, _, _ in chunks:
        row += log_beta.shape[0]
        ends.append(row)
    ws = [None] * len(chunks)
    for ci in reversed(range(len(chunks))):
        log_beta, vis, row0 = chunks[ci]
        rc = r[ends[ci] - log_beta.shape[0]:ends[ci]]
        w = jnp.exp(log_beta + rc[:, :cw] + c[row0:])
        if vis is not None:
            w = jnp.where(vis, w, 0.0)
        w = w.astype(BF16)
        c_new = c[row0:] + rc[:, cw:]
        if row0:
            w = jnp.concatenate([jnp.zeros((row0, cw), BF16), w], axis=0)
            c_new = jnp.concatenate([c[:row0], c_new], axis=0)
        ws[ci] = w
        c = c_new
    w = ws[0] if len(ws) == 1 else jnp.concatenate(ws, axis=1)
    return jnp.dot(w, v, preferred_element_type=F32), c


def _sb_prompt_kernel(q_ref, k_ref, v_ref, o_ref, acc_scr, c_scr, *, bq, n_par):
    n_blocks = q_ref.shape[0] // bq
    n_groups = (n_blocks - 1) // n_par
    uu = _cumsum_ones(LANES)
    c0 = jnp.zeros((bq, LANES), F32)

    acc, _ = _sb_tile(q_ref[0:bq, :], k_ref[0:bq, :], v_ref[0:bq, :], c0, uu, 0)
    o_ref[0:bq, :] = acc.astype(o_ref.dtype)

    def group(g, carry):
        items, q0s = [], []
        for u in range(n_par):
            q0 = pl.multiple_of((1 + g + u * n_groups) * bq, bq)
            k0 = pl.multiple_of(q0 - bq, bq)
            items.append((q_ref[pl.ds(q0, bq), :], k_ref[pl.ds(k0, 2 * bq), :], v_ref[pl.ds(k0, 2 * bq), :], c0))
            q0s.append(q0)
        res = _sb_tiles(items, uu, bq)
        for q0, (acc, _) in zip(q0s, res):
            o_ref[pl.ds(q0, bq), :] = acc.astype(o_ref.dtype)
        c_max = res[0][1]
        for _, c in res[1:]:
            c_max = jnp.maximum(c_max, c)

        @pl.when(jnp.max(c_max) > LOG_WEIGHT_ZERO)
        def _():
            for u, (acc, c) in enumerate(res):
                acc_scr[u] = acc
                c_scr[u] = c

            def older_keys(u, carry):
                qb = 1 + g + u * n_groups
                q0 = pl.multiple_of(qb * bq, bq)
                k0 = pl.multiple_of(q0 - bq, bq)
                q = q_ref[pl.ds(q0, bq), :]
                n_wide = (qb - 1) // 2

                def more(st):
                    j, _, c = st
                    return jnp.logical_and(j < n_wide, jnp.max(c) > LOG_WEIGHT_ZERO)

                def wide_tile(st):
                    j, acc, c = st
                    ks = pl.multiple_of(k0 - 2 * bq * (j + 1), bq)
                    da, c = _sb_tile(q, k_ref[pl.ds(ks, 2 * bq), :], v_ref[pl.ds(ks, 2 * bq), :], c, uu, None)
                    return j + 1, acc + da, c

                j, acc, c = lax.while_loop(more, wide_tile, (jnp.int32(0), acc_scr[u], c_scr[u]))
                need_last = jnp.logical_and(jnp.logical_and((qb - 1) % 2 == 1, j == n_wide),
                                            jnp.max(c) > LOG_WEIGHT_ZERO)

                def last_tile(acc, c):
                    da, _ = _sb_tile(q, k_ref[0:bq, :], v_ref[0:bq, :], c, uu, None)
                    return acc + da

                acc = lax.cond(need_last, last_tile, lambda acc, c: acc, acc, c)
                o_ref[pl.ds(q0, bq), :] = acc.astype(o_ref.dtype)
                return carry

            lax.fori_loop(0, n_par, older_keys, 0)

        return carry

    lax.fori_loop(0, n_groups, group, 0)


def _sb_prompt(q, k, v):
    h, t, dh = q.shape
    bq = _pick(t, BQ_ATT)
    n_rest = t // bq - 1
    n_par = next(u for u in (7, 4, 3, 2, 1) if n_rest % u == 0)
    spec = pl.BlockSpec((None, t, dh), lambda i: (i, 0, 0))
    return pl.pallas_call(
        functools.partial(_sb_prompt_kernel, bq=bq, n_par=n_par),
        grid=(h,),
        in_specs=[spec, spec, spec],
        out_specs=spec,
        out_shape=jax.ShapeDtypeStruct((h, t, dh), BF16),
        scratch_shapes=[pltpu.VMEM((n_par, bq, dh), F32), pltpu.VMEM((n_par, bq, LANES), F32)],
        compiler_params=_cparams("parallel"),
        name="sb_prompt",
    )(q, k, v)


def _sb_sample_kernel(q_ref, kn_ref, vn_ref, kp_ref, vp_ref, o_ref, acc_scr, c_scr, *, tk):
    n_heads, tq, dh = q_ref.shape
    kt = pl.program_id(1)
    uu = _cumsum_ones(LANES)

    @pl.when(kt == 0)
    def _():
        pad = jnp.zeros((LANES - tq, dh), BF16)
        c0 = jnp.zeros((tq, LANES), F32)
        items = [(q_ref[h], jnp.concatenate([kn_ref[h], pad], axis=0),
                  jnp.concatenate([vn_ref[h], pad], axis=0), c0) for h in range(n_heads)]
        for h, (acc, c) in enumerate(_sb_tiles(items, uu, 0)):
            acc_scr[h] = acc
            c_scr[h] = c

    items = [(q_ref[h], kp_ref[0, pl.ds(h, tk, stride=n_heads), :].astype(BF16),
              vp_ref[0, pl.ds(h, tk, stride=n_heads), :].astype(BF16), c_scr[h]) for h in range(n_heads)]
    for h, (da, c) in enumerate(_sb_tiles(items, uu, None)):
        acc_scr[h] = acc_scr[h] + da
        c_scr[h] = c

    @pl.when(kt == pl.num_programs(1) - 1)
    def _():
        for h in range(n_heads):
            o_ref[h] = acc_scr[h].astype(o_ref.dtype)


def _sb_sample(q, kn, vn, kp, vp, tq):
    h, m, dh = q.shape
    bsz = m // tq
    p = kp.shape[1] // h
    tk = _pick(p, TK_SAMPLE)
    n_kt = p // tk
    assert tq <= LANES and tk % LANES == 0
    new = pl.BlockSpec((h, tq, dh), lambda b, kt: (0, b, 0))
    past = pl.BlockSpec((1, tk * h, dh), lambda b, kt: (b, n_kt - 1 - kt, 0))
    return pl.pallas_call(
        functools.partial(_sb_sample_kernel, tk=tk),
        grid=(bsz, n_kt),
        in_specs=[new, new, new, past, past],
        out_specs=new,
        out_shape=jax.ShapeDtypeStruct((h, m, dh), BF16),
        scratch_shapes=[pltpu.VMEM((h, tq, dh), F32), pltpu.VMEM((h, tq, LANES), F32)],
        compiler_params=_cparams("parallel", "arbitrary"),
        name="sb_sample",
    )(q, kn, vn, kp, vp)


def _shift_rows(x, prev8, s):
    n, c = x.shape
    groups = n // SUBLANES
    rot = pltpu.roll(x.reshape(groups, SUBLANES, c), s, 1)
    before = pltpu.roll(prev8, s, 0).reshape(1, SUBLANES, c)
    if groups > 1:
        before = jnp.concatenate([before, rot[:groups - 1]], axis=0)
    sub = lax.broadcasted_iota(jnp.int32, rot.shape, 1)
    return jnp.where(sub < s, before, rot).reshape(n, c)


def _causal_conv(x, prev8, w, b):
    width = w.shape[0]
    y = b + w[width - 1:width] * x
    for j in range(width - 1):
        y = y + w[j:j + 1] * _shift_rows(x, prev8, width - 1 - j)
    return y


def _linear_scan(a, u, h_in):
    n, c = a.shape
    a = a.reshape(n // SUBLANES, SUBLANES, c)
    u = u.reshape(n // SUBLANES, SUBLANES, c)
    sub = lax.broadcasted_iota(jnp.int32, a.shape, 1)
    d = 1
    while d < SUBLANES:
        keep = sub >= d
        a_sh = jnp.where(keep, pltpu.roll(a, d, 1), 1.0)
        u_sh = jnp.where(keep, pltpu.roll(u, d, 1), 0.0)
        u = u + a * u_sh
        a = a * a_sh
        d *= 2
    a = a.reshape(n, c)
    u = u.reshape(n, c)
    hs = []
    for g in range(n // SUBLANES):
        rows = slice(g * SUBLANES, (g + 1) * SUBLANES)
        hg = a[rows] * h_in + u[rows]
        hs.append(hg)
        h_in = hg[SUBLANES - 1:SUBLANES, :]
    return hs[0] if len(hs) == 1 else jnp.concatenate(hs, axis=0)


def _lru_kernel(xl_ref, gl_ref, prev_ref, h0_ref, cw_ref, cb_ref, wa_ref, ba_ref, wx_ref, bx_ref, lam_ref,
                o_ref, conv8_ref, h8_ref, prev_scr, h_scr, *, n_seq, piece):
    t = pl.program_id(1)

    @pl.when(t == 0)
    def _():
        prev_scr[...] = prev_ref[...]
        h_scr[...] = h0_ref[...]

    lam = lam_ref[...]
    softplus_neg_lam = jnp.maximum(-lam, 0.0) + jnp.log1p(jnp.exp(-jnp.abs(lam)))
    wa = wa_ref[0]
    wx = wx_ref[0]
    seq_rows = xl_ref.shape[0] // n_seq
    for s in range(n_seq):
        prev8 = prev_scr[s]
        h8 = h_scr[s]
        for pc in range(seq_rows // piece):
            rows = slice(s * seq_rows + pc * piece, s * seq_rows + (pc + 1) * piece)
            x = xl_ref[rows, :]
            xc = _causal_conv(x, prev8, cw_ref[...], cb_ref[...])
            xcb = xc.astype(BF16)
            r = jax.nn.sigmoid(jnp.dot(xcb, wa, preferred_element_type=F32) + ba_ref[...])
            i = jax.nn.sigmoid(jnp.dot(xcb, wx, preferred_element_type=F32) + bx_ref[...])
            log_a = -LRU_C * r * softplus_neg_lam
            a = jnp.exp(log_a)
            u = jnp.sqrt(-jnp.tanh(log_a) * (a * a + 1.0)) * (i * xc)
            hs = _linear_scan(a, u, h8[SUBLANES - 1:SUBLANES, :])
            o_ref[rows, :] = (hs * _gelu_tanh(gl_ref[rows, :])).astype(o_ref.dtype)
            prev8 = x[piece - SUBLANES:, :]
            h8 = hs[piece - SUBLANES:, :]
        prev_scr[s] = prev8
        h_scr[s] = h8
        conv8_ref[s] = prev8
        h8_ref[s] = h8


def _lru(xg, prev8, h08, conv_w, conv_b, w_a, b_a, w_x, b_x, lam, n_seq, seq_len):
    m, c2 = xg.shape
    c = c2 // 2
    nb = c // LANES
    assert w_a.shape == (nb, LANES, LANES) and seq_len % SUBLANES == 0 and seq_len >= conv_w.shape[0] - 1
    if n_seq == 1:
        tt = _pick(seq_len, TT_LRU)
        s_t, piece = 1, _pick(tt, PC_LRU)
    else:
        tt = m
        s_t, piece = n_seq, seq_len
    nt = m // tt
    width = conv_w.shape[0]
    vec = pl.BlockSpec((1, LANES), lambda n, t: (0, n))
    st8 = pl.BlockSpec((s_t, SUBLANES, LANES), lambda n, t: (0, 0, n))
    out8 = pl.BlockSpec((s_t, SUBLANES, LANES), lambda n, t: (t, 0, n))
    gate_w = pl.BlockSpec((1, LANES, LANES), lambda n, t: (n, 0, 0))
    return pl.pallas_call(
        functools.partial(_lru_kernel, n_seq=s_t, piece=piece),
        grid=(nb, nt),
        in_specs=[pl.BlockSpec((tt, LANES), lambda n, t: (t, n)),
                  pl.BlockSpec((tt, LANES), lambda n, t: (t, n + nb)),
                  st8, st8,
                  pl.BlockSpec((width, LANES), lambda n, t: (0, n)), vec,
                  gate_w, vec, gate_w, vec, vec],
        out_specs=[pl.BlockSpec((tt, LANES), lambda n, t: (t, n)), out8, out8],
        out_shape=[jax.ShapeDtypeStruct((m, c), BF16),
                   jax.ShapeDtypeStruct((nt * s_t, SUBLANES, c), F32),
                   jax.ShapeDtypeStruct((nt * s_t, SUBLANES, c), F32)],
        scratch_shapes=[pltpu.VMEM((s_t, SUBLANES, LANES), F32), pltpu.VMEM((s_t, SUBLANES, LANES), F32)],
        compiler_params=_cparams("parallel", "arbitrary"),
        name="rg_lru",
    )(xg, xg, prev8, h08, conv_w, conv_b, w_a, b_a, w_x, b_x, lam)


def _ffn_up_kernel(h_ref, wg_ref, wv_ref, pg_ref, pv_ref, cwg_ref, cwv_ref, cbg_ref, cbv_ref,
                   act_ref, lastg_ref, lastv_ref, raw_g, raw_v, carry_g, carry_v, *, n_seq, nj):
    s = pl.program_id(0)
    tile_e = jnp.maximum(s - 1, 0)
    i_e = lax.div(tile_e, nj)
    j_e = lax.rem(tile_e, nj)

    @pl.when(s == 0)
    def _():
        raw_g[1] = jnp.zeros(raw_g.shape[1:], F32)
        raw_v[1] = jnp.zeros(raw_v.shape[1:], F32)

    @pl.when(i_e == 0)
    def _():
        carry_g[j_e] = pg_ref[...]
        carry_v[j_e] = pv_ref[...]

    valid = s > 0
    tm = h_ref.shape[0]
    seq_rows = tm // n_seq
    piece = min(seq_rows, PIECE_FFN)

    def step(cur, prv):
        pieces = [(sq, p0) for sq in range(n_seq) for p0 in range(sq * seq_rows, (sq + 1) * seq_rows, piece)]
        n_k = max(1, min(KCHUNKS_FFN, len(pieces), h_ref.shape[1] // MXU_DEPTH))
        kc_size = h_ref.shape[1] // n_k
        prev_g = [carry_g[j_e, sq] for sq in range(n_seq)]
        prev_v = [carry_v[j_e, sq] for sq in range(n_seq)]
        old_g, old_v = list(prev_g), list(prev_v)
        acc_g = acc_v = None
        done = 0
        for kc in range(n_k):
            ks = slice(kc * kc_size, (kc + 1) * kc_size)
            h = h_ref[:, ks]
            part_g = jnp.dot(h, wg_ref[ks, :].astype(BF16), preferred_element_type=F32)
            part_v = jnp.dot(h, wv_ref[ks, :].astype(BF16), preferred_element_type=F32)
            acc_g = part_g if acc_g is None else acc_g + part_g
            acc_v = part_v if acc_v is None else acc_v + part_v
            upto = len(pieces) * (kc + 1) // n_k
            for sq, p0 in pieces[done:upto]:
                xg = raw_g[prv, p0:p0 + piece, :]
                xv = raw_v[prv, p0:p0 + piece, :]
                yg = _causal_conv(xg, prev_g[sq], cwg_ref[...], cbg_ref[...])
                yv = _causal_conv(xv, prev_v[sq], cwv_ref[...], cbv_ref[...])
                act_ref[p0:p0 + piece, :] = (_gelu_tanh(yg) * yv).astype(act_ref.dtype)
                prev_g[sq] = xg[piece - SUBLANES:, :]
                prev_v[sq] = xv[piece - SUBLANES:, :]
            done = upto
        raw_g[cur] = acc_g
        raw_v[cur] = acc_v
        for sq in range(n_seq):
            carry_g[j_e, sq] = jnp.where(valid, prev_g[sq], old_g[sq])
            carry_v[j_e, sq] = jnp.where(valid, prev_v[sq], old_v[sq])
            lastg_ref[sq] = prev_g[sq]
            lastv_ref[sq] = prev_v[sq]

    slot = lax.rem(s, 2)
    pl.when(slot == 0)(lambda: step(0, 1))
    pl.when(slot == 1)(lambda: step(1, 0))


def _ffn_up(h, w_up, prev8, conv_w, conv_b, n_seq, seq_len):
    m, d = h.shape
    f = w_up.shape[1] // 2
    tn = _pick(f, TN_FFN)
    assert seq_len % SUBLANES == 0 and seq_len >= conv_w.shape[0] - 1
    if n_seq == 1:
        tm = _pick(seq_len, TM_FFN)
        s_t = 1
    else:
        tm = m
        s_t = n_seq
    ni, nj = m // tm, f // tn
    n_tiles = ni * nj
    width = conv_w.shape[0]
    i_d = lambda s: jnp.minimum(s, n_tiles - 1) // nj
    j_d = lambda s: jnp.minimum(s, n_tiles - 1) % nj
    i_e = lambda s: jnp.maximum(s - 1, 0) // nj
    j_e = lambda s: jnp.maximum(s - 1, 0) % nj
    st8 = lambda off: pl.BlockSpec((s_t, SUBLANES, tn), lambda s: (0, 0, j_e(s) + off))
    out8 = pl.BlockSpec((s_t, SUBLANES, tn), lambda s: (i_e(s), 0, j_e(s)))
    wspec = lambda off: pl.BlockSpec((d, tn), lambda s: (0, j_d(s) + off))
    cw = lambda off: pl.BlockSpec((width, tn), lambda s: (0, j_e(s) + off))
    cb = lambda off: pl.BlockSpec((1, tn), lambda s: (0, j_e(s) + off))
    return pl.pallas_call(
        functools.partial(_ffn_up_kernel, n_seq=s_t, nj=nj),
        grid=(n_tiles + 1,),
        in_specs=[pl.BlockSpec((tm, d), lambda s: (i_d(s), 0)),
                  wspec(0), wspec(nj), st8(0), st8(nj), cw(0), cw(nj), cb(0), cb(nj)],
        out_specs=[pl.BlockSpec((tm, tn), lambda s: (i_e(s), j_e(s))), out8, out8],
        out_shape=[jax.ShapeDtypeStruct((m, f), BF16),
                   jax.ShapeDtypeStruct((ni * s_t, SUBLANES, f), F32),
                   jax.ShapeDtypeStruct((ni * s_t, SUBLANES, f), F32)],
        scratch_shapes=[pltpu.VMEM((2, tm, tn), F32), pltpu.VMEM((2, tm, tn), F32),
                        pltpu.VMEM((nj, s_t, SUBLANES, tn), F32), pltpu.VMEM((nj, s_t, SUBLANES, tn), F32)],
        compiler_params=_cparams("arbitrary"),
        name="ffn_up",
    )(h, w_up, w_up, prev8, prev8, conv_w, conv_w, conv_b, conv_b)


def _state8(state, n_rows):
    return jnp.pad(state.astype(F32), ((0, 0), (SUBLANES - n_rows, 0), (0, 0)))


def _layer(x, mods, p, past, n_seq, seq_len):
    m, d = x.shape
    d_sb = d // 2
    d_lru = d // 2
    sh_m, sc_m, g_m, sh_f, sc_f, g_f = mods
    k_past, v_past, conv_lru_prev, h_prev, conv_ffn_prev = past

    h = _norm_mod(x, p['g_pre_mix'], sc_m, sh_m)
    (q,) = _matmul(h, p['w_in'], 0, d_sb, [(BF16, True)], name="proj_q")
    k32, k16 = _matmul(h, p['w_in'], d_sb, d_sb, [(F32, False), (BF16, True)], name="proj_k")
    v32, v16 = _matmul(h, p['w_in'], 2 * d_sb, d_sb, [(F32, False), (BF16, True)], name="proj_v")
    (xg,) = _matmul(h, p['w_in'], 3 * d_sb, 2 * d_lru, [(F32, False)], name="proj_lru")

    if k_past is None:
        o_sb = _sb_prompt(q, k16, v16)
    else:
        o_sb = _sb_sample(q, k16, v16, k_past, v_past, seq_len)

    w_c = p['w_conv_lru']
    o_lru, conv8, h8 = _lru(xg, _state8(conv_lru_prev, w_c.shape[0] - 1),
                            jnp.broadcast_to(h_prev.astype(F32)[:, None, :], (n_seq, SUBLANES, d_lru)),
                            w_c, p['b_conv_lru'], p['w_gate_a'], p['b_gate_a'], p['w_gate_x'], p['b_gate_x'],
                            p['lru_lambda'], n_seq, seq_len)
    conv_lru_new = conv8[-n_seq:, SUBLANES - (w_c.shape[0] - 1):, :]
    h_last = h8[-n_seq:, SUBLANES - 1, :]

    mix_in = _group_norm_concat(o_sb, o_lru, p['g_grp_attn'], p['g_grp_lru'])
    (mixed,) = _matmul(mix_in, p['w_out'], 0, d, [(BF16, False)], name="proj_out")
    x1, h2 = _post_mix(x, mixed, p['g_post_mix'], g_m, p['g_pre_ffn'], sc_f, sh_f)

    w_f = p['w_conv_ffn']
    act, last_g, last_v = _ffn_up(h2, p['w_up'], _state8(conv_ffn_prev, w_f.shape[0] - 1),
                                  w_f, p['b_conv_ffn'], n_seq, seq_len)
    keep = SUBLANES - (w_f.shape[0] - 1)
    conv_ffn_new = jnp.concatenate([last_g[-n_seq:, keep:, :], last_v[-n_seq:, keep:, :]], axis=-1)
    out = _ffn_down_residual(act, p['w_down'], x1, p['g_post_ffn'], g_f)
    return out, (k32, v32, conv_lru_new, h_last, conv_ffn_new)


def kernel(x_prompt, x_sample, c_prompt, c_sample, cache_k, cache_v, state_conv_lru, state_lru, state_conv_ffn, w_ada, b_ada, g_pre_mix, w_in, w_conv_lru, b_conv_lru, w_gate_a, b_gate_a, w_gate_x, b_gate_x, lru_lambda, g_grp_attn, g_grp_lru, w_out, g_post_mix, g_pre_ffn, w_up, w_conv_ffn, b_conv_ffn, w_down, g_post_ffn):
    depth = w_ada.shape[0]
    bp, tp, d = x_prompt.shape
    bs, ts, _ = x_sample.shape
    assert bp == 1, "the prompt kernels treat the prompt as one sequence"
    d_sb = d // 2
    n_heads = d_sb // HEAD_DIM
    n_mod = w_ada.shape[2] // d

    xp = x_prompt.reshape(bp * tp, d)
    xs = x_sample.reshape(bs * ts, d)
    n_c = bp + bs
    n_c_pad = -(-n_c // SUBLANES) * SUBLANES
    c_all = jnp.concatenate([c_prompt, c_sample, jnp.zeros((n_c_pad - n_c, d), F32)], axis=0)

    st_p = [[], [], [], [], []]
    st_s = [[], [], [], [], []]
    for l in range(depth):
        row = lambda a: a[l][None, :]
        p = {
            'g_pre_mix': row(g_pre_mix), 'w_in': w_in[l].astype(BF16),
            'w_conv_lru': w_conv_lru[l], 'b_conv_lru': row(b_conv_lru),
            'w_gate_a': w_gate_a[l].astype(BF16), 'b_gate_a': row(b_gate_a),
            'w_gate_x': w_gate_x[l].astype(BF16), 'b_gate_x': row(b_gate_x),
            'lru_lambda': row(lru_lambda), 'g_grp_attn': row(g_grp_attn), 'g_grp_lru': row(g_grp_lru),
            'w_out': w_out[l].astype(BF16), 'g_post_mix': row(g_post_mix), 'g_pre_ffn': row(g_pre_ffn),
            'w_up': w_up[l], 'w_conv_ffn': w_conv_ffn[l], 'b_conv_ffn': row(b_conv_ffn),
            'w_down': w_down[l].astype(BF16), 'g_post_ffn': row(g_post_ffn),
        }
        mod = _ada(c_all, w_ada[l], row(b_ada))
        mods_p = [mod[0:1, i * d:(i + 1) * d] for i in range(n_mod)]
        mods_s = [jnp.repeat(mod[bp:bp + bs, i * d:(i + 1) * d], ts, axis=0) for i in range(n_mod)]

        d_lru = d // 2
        f2 = w_up.shape[2]
        prompt_past = (None, None, jnp.zeros((bp, w_conv_lru.shape[1] - 1, d_lru), F32),
                       jnp.zeros((bp, d_lru), F32), jnp.zeros((bp, w_conv_ffn.shape[1] - 1, f2), F32))
        xp, new_p = _layer(xp, mods_p, p, prompt_past, bp, tp)
        past_len = cache_k.shape[2]
        sample_past = (cache_k[l].reshape(bs, past_len * n_heads, HEAD_DIM),
                       cache_v[l].reshape(bs, past_len * n_heads, HEAD_DIM),
                       state_conv_lru[l], state_lru[l], state_conv_ffn[l])
        xs, new_s = _layer(xs, mods_s, p, sample_past, bs, ts)
        for j in range(5):
            st_p[j].append(new_p[j])
            st_s[j].append(new_s[j])

    def pack(st, b, t):
        k = jnp.stack(st[0], axis=0).reshape(depth, b, t, n_heads, HEAD_DIM)
        v = jnp.stack(st[1], axis=0).reshape(depth, b, t, n_heads, HEAD_DIM)
        return k, v, jnp.stack(st[2], axis=0), jnp.stack(st[3], axis=0), jnp.stack(st[4], axis=0)

    k_p, v_p, cl_p, h_p, cf_p = pack(st_p, bp, tp)
    k_s, v_s, cl_s, h_s, cf_s = pack(st_s, bs, ts)
    return (xp.reshape(bp, tp, d), xs.reshape(bs, ts, d), k_p, v_p, cl_p, h_p, cf_p, k_s, v_s, cl_s, h_s, cf_s)
```

```python
import functools
import math

import jax
import jax.numpy as jnp
from jax import lax
from jax.experimental import pallas as pl
from jax.experimental.pallas import tpu as pltpu

F32 = jnp.float32
BF16 = jnp.bfloat16

EPS = 1e-6
LRU_C = 8.0
HEAD_DIM = 128
LANES = 128
SUBLANES = 8
MXU_DEPTH = 256
BF16_ROWS = 16
VMEM_LIMIT = 56 * 1024 * 1024
LOG_WEIGHT_ZERO = -110.0

TM_ROW = 256
TM_MM = 1024
TN_MM = 1024
TN_ADA = 512
TM_FFN = 1024
TN_FFN = 256
PIECE_FFN = 64
KCHUNKS_FFN = 16
TM_DOWN = 512
TN_DOWN = 512
KCHUNKS_DOWN = 4
TT_LRU = 1024
PC_LRU = 256
BQ_ATT = 256
TK_SAMPLE = 512


def _pick(n, pref):
    if n <= pref:
        return n
    t = pref
    while n % t:
        t //= 2
    assert t >= SUBLANES, (n, pref)
    return t


def _cparams(*sem):
    return pltpu.CompilerParams(dimension_semantics=sem, vmem_limit_bytes=VMEM_LIMIT)


def _gelu_tanh(x):
    return 0.5 * x * (1.0 + jnp.tanh(math.sqrt(2.0 / math.pi) * (x + 0.044715 * (x * x * x))))


def _rms(x, g):
    ms = jnp.mean(x * x, axis=-1, keepdims=True)
    return x * lax.rsqrt(ms + EPS) * g


def _row_tile(m, *mods):
    per_row = any(mod.shape[0] != 1 for mod in mods)
    return _pick(m, TM_ROW // 2 if per_row else TM_ROW)


def _mod_spec(mod, tm, d):
    if mod.shape[0] == 1:
        return pl.BlockSpec((1, d), lambda i: (0, 0))
    return pl.BlockSpec((tm, d), lambda i: (i, 0))


def _ada_kernel(c_ref, w_ref, b_ref, o_ref):
    c = c_ref[...]
    a = (c * jax.nn.sigmoid(c)).astype(BF16)
    o_ref[...] = jnp.dot(a, w_ref[...].astype(BF16), preferred_element_type=F32) + b_ref[...]


def _ada(c, w, b):
    bsz, d = c.shape
    n = w.shape[1]
    tn = _pick(n, TN_ADA)
    return pl.pallas_call(
        _ada_kernel,
        grid=(n // tn,),
        in_specs=[pl.BlockSpec((bsz, d), lambda j: (0, 0)),
                  pl.BlockSpec((d, tn), lambda j: (0, j)),
                  pl.BlockSpec((1, tn), lambda j: (0, j))],
        out_specs=pl.BlockSpec((bsz, tn), lambda j: (0, j)),
        out_shape=jax.ShapeDtypeStruct((bsz, n), F32),
        compiler_params=_cparams("parallel"),
        name="ada_proj",
    )(c, w, b)


def _norm_mod_kernel(x_ref, g_ref, sc_ref, sh_ref, o_ref):
    y = _rms(x_ref[...], g_ref[...])
    o_ref[...] = (y * (1.0 + sc_ref[...]) + sh_ref[...]).astype(o_ref.dtype)


def _norm_mod(x, g, sc, sh):
    m, d = x.shape
    tm = _row_tile(m, sc, sh)
    return pl.pallas_call(
        _norm_mod_kernel,
        grid=(m // tm,),
        in_specs=[pl.BlockSpec((tm, d), lambda i: (i, 0)),
                  pl.BlockSpec((1, d), lambda i: (0, 0)),
                  _mod_spec(sc, tm, d), _mod_spec(sh, tm, d)],
        out_specs=pl.BlockSpec((tm, d), lambda i: (i, 0)),
        out_shape=jax.ShapeDtypeStruct((m, d), BF16),
        compiler_params=_cparams("parallel"),
        name="norm_mod",
    )(x, g, sc, sh)


def _group_norm_kernel(a_ref, b_ref, ga_ref, gb_ref, o_ref):
    a = jnp.concatenate([a_ref[h] for h in range(a_ref.shape[0])], axis=1).astype(F32)
    da = a.shape[1]
    o_ref[:, :da] = _rms(a, ga_ref[...]).astype(o_ref.dtype)
    o_ref[:, da:] = _rms(b_ref[...].astype(F32), gb_ref[...]).astype(o_ref.dtype)


def _group_norm_concat(a, b, ga, gb):
    h, m, dh = a.shape
    da = h * dh
    db = b.shape[1]
    tm = _pick(m, TM_ROW)
    return pl.pallas_call(
        _group_norm_kernel,
        grid=(m // tm,),
        in_specs=[pl.BlockSpec((h, tm, dh), lambda i: (0, i, 0)),
                  pl.BlockSpec((tm, db), lambda i: (i, 0)),
                  pl.BlockSpec((1, da), lambda i: (0, 0)),
                  pl.BlockSpec((1, db), lambda i: (0, 0))],
        out_specs=pl.BlockSpec((tm, da + db), lambda i: (i, 0)),
        out_shape=jax.ShapeDtypeStruct((m, da + db), BF16),
        compiler_params=_cparams("parallel"),
        name="group_norm_concat",
    )(a, b, ga, gb)


def _post_mix_kernel(x_ref, y_ref, gpost_ref, gate_ref, gpre_ref, sc_ref, sh_ref, x1_ref, h_ref):
    x1 = x_ref[...] + gate_ref[...] * _rms(y_ref[...].astype(F32), gpost_ref[...])
    x1_ref[...] = x1
    h = _rms(x1, gpre_ref[...])
    h_ref[...] = (h * (1.0 + sc_ref[...]) + sh_ref[...]).astype(h_ref.dtype)


def _post_mix(x, y, gpost, gate, gpre, sc, sh):
    m, d = x.shape
    tm = _row_tile(m, gate, sc, sh)
    row = pl.BlockSpec((tm, d), lambda i: (i, 0))
    vec = pl.BlockSpec((1, d), lambda i: (0, 0))
    return pl.pallas_call(
        _post_mix_kernel,
        grid=(m // tm,),
        in_specs=[row, row, vec, _mod_spec(gate, tm, d), vec, _mod_spec(sc, tm, d), _mod_spec(sh, tm, d)],
        out_specs=[row, row],
        out_shape=[jax.ShapeDtypeStruct((m, d), F32), jax.ShapeDtypeStruct((m, d), BF16)],
        compiler_params=_cparams("parallel"),
        name="post_mix",
    )(x, y, gpost, gate, gpre, sc, sh)


def _ffn_down_kernel(a_ref, b_ref, x_ref, gpost_ref, gate_ref, o_ref, y_scr, ssq_scr, *, nj, ni):
    s = pl.program_id(0)
    row = lax.div(s, nj)
    j = lax.rem(s, nj)
    cur = lax.rem(row, 2)
    tm, tn = o_ref.shape

    @pl.when(s == 0)
    def _():
        y_scr[...] = jnp.zeros(y_scr.shape, y_scr.dtype)
        ssq_scr[...] = jnp.zeros(ssq_scr.shape, F32)

    def residual(rows):
        ms = ssq_scr[1 - cur, rows, 0:1] * (1.0 / (nj * tn))
        gate = gate_ref[rows, :] if gate_ref.shape[0] > 1 else gate_ref[...]
        y_prev = y_scr[j, rows, :].astype(F32)
        o_ref[rows, :] = x_ref[rows, :] + gate * (y_prev * lax.rsqrt(ms + EPS) * gpost_ref[...])

    k = a_ref.shape[1]
    n_k = 1
    while 2 * n_k <= min(KCHUNKS_DOWN, k // MXU_DEPTH) and tm % (2 * n_k * SUBLANES) == 0:
        n_k *= 2
    bounds = [k // MXU_DEPTH * c // n_k * MXU_DEPTH for c in range(n_k)] + [k]
    rows_per = tm // n_k

    @pl.when(row < ni)
    def _():
        y = None
        for c in range(n_k):
            ks = slice(bounds[c], bounds[c + 1])
            part = jnp.dot(a_ref[:, ks], b_ref[ks, :], preferred_element_type=F32)
            y = part if y is None else y + part
            residual(slice(c * rows_per, (c + 1) * rows_per))
        y_scr[j] = y.astype(y_scr.dtype)
        sq = jnp.broadcast_to(jnp.sum(y * y, axis=1, keepdims=True), (tm, LANES))
        ssq_scr[cur] = jnp.where(j == 0, sq, ssq_scr[cur] + sq)

    @pl.when(row == ni)
    def _():
        residual(slice(0, tm))


def _ffn_down_residual(a, b, x, gpost, gate):
    m, k = a.shape
    n = b.shape[1]
    tm = _pick(m, TM_DOWN)
    tn = _pick(n, TN_DOWN)
    ni, nj = m // tm, n // tn
    r_d = lambda s: jnp.minimum(s // nj, ni - 1)
    r_e = lambda s: jnp.maximum(s // nj - 1, 0)
    j_e = lambda s: jnp.where(s < nj, 0, s % nj)
    if gate.shape[0] == 1:
        gate_spec = pl.BlockSpec((1, tn), lambda s: (0, j_e(s)))
    else:
        gate_spec = pl.BlockSpec((tm, tn), lambda s: (r_e(s), j_e(s)))
    return pl.pallas_call(
        functools.partial(_ffn_down_kernel, nj=nj, ni=ni),
        grid=((ni + 1) * nj,),
        in_specs=[pl.BlockSpec((tm, k), lambda s: (r_d(s), 0)),
                  pl.BlockSpec((k, tn), lambda s: (0, jnp.where(s // nj < ni, s % nj, nj - 1))),
                  pl.BlockSpec((tm, tn), lambda s: (r_e(s), j_e(s))),
                  pl.BlockSpec((1, tn), lambda s: (0, j_e(s))),
                  gate_spec],
        out_specs=pl.BlockSpec((tm, tn), lambda s: (r_e(s), j_e(s))),
        out_shape=jax.ShapeDtypeStruct((m, n), F32),
        scratch_shapes=[pltpu.VMEM((nj, tm, tn), BF16), pltpu.VMEM((2, tm, LANES), F32)],
        compiler_params=_cparams("arbitrary"),
        name="ffn_down",
    )(a, b, x, gpost, gate)


def _matmul_kernel(a_ref, b_ref, *o_refs):
    acc = jnp.dot(a_ref[...], b_ref[...], preferred_element_type=F32)
    for o_ref in o_refs:
        if len(o_ref.shape) == 3:
            for h in range(o_ref.shape[0]):
                o_ref[h] = acc[:, h * LANES:(h + 1) * LANES].astype(o_ref.dtype)
        else:
            o_ref[...] = acc.astype(o_ref.dtype)


def _matmul(a, b, col0, n, outs, tm_pref=None, tn_pref=None, name="matmul"):
    m, k = a.shape
    tm = _pick(m, TM_MM if tm_pref is None else tm_pref)
    tn = _pick(n, TN_MM if tn_pref is None else tn_pref)
    assert col0 % tn == 0 and tn % LANES == 0
    off = col0 // tn
    hpt = tn // LANES
    out_specs, out_shape = [], []
    for dt, head_major in outs:
        if head_major:
            out_specs.append(pl.BlockSpec((hpt, tm, LANES), lambda i, j: (j, i, 0)))
            out_shape.append(jax.ShapeDtypeStruct((n // LANES, m, LANES), dt))
        else:
            out_specs.append(pl.BlockSpec((tm, tn), lambda i, j: (i, j)))
            out_shape.append(jax.ShapeDtypeStruct((m, n), dt))
    return pl.pallas_call(
        _matmul_kernel,
        grid=(m // tm, n // tn),
        in_specs=[pl.BlockSpec((tm, k), lambda i, j: (i, 0)),
                  pl.BlockSpec((k, tn), lambda i, j: (0, j + off))],
        out_specs=out_specs,
        out_shape=out_shape,
        compiler_params=_cparams("parallel", "arbitrary"),
        name=name,
    )(a, b)


def _cumsum_ones(cw):
    j = lax.broadcasted_iota(jnp.int32, (2 * cw, 2 * cw), 0)
    s = lax.broadcasted_iota(jnp.int32, (2 * cw, 2 * cw), 1)
    jj = jnp.where(j >= cw, j - cw, j)
    return jnp.where((s >= cw) | (jj > s), 1.0, 0.0).astype(BF16)


def _sb_tile(q, k, v, c, uu, mask_offset):
    return _sb_tiles([(q, k, v, c)], uu, mask_offset)[0]


def _sb_tiles(items, uu, mask_offset):
    scored = [_sb_scores(q, k, uu.shape[0] // 2, mask_offset) for q, k, _, _ in items]
    xs = [x for _, x in scored]
    r = jnp.dot(xs[0] if len(xs) == 1 else jnp.concatenate(xs, axis=0), uu, preferred_element_type=F32)
    outs, row = [], 0
    for (_, _, v, c), (chunks, x) in zip(items, scored):
        outs.append(_sb_weights(chunks, r[row:row + x.shape[0]], c, v))
        row += x.shape[0]
    return outs


def _sb_scores(q, k, cw, mask_offset):
    bq, dh = q.shape
    nc = k.shape[0] // cw
    z = lax.dot_general(q, k, (((1,), (1,)), ((), ())), preferred_element_type=F32) * (dh ** -0.5)
    chunks, xs = [], []
    for ci in range(nc):
        masked = mask_offset is not None and (ci + 1) * cw > mask_offset
        row0 = 0
        if masked:
            row0 = min(max(ci * cw - mask_offset + 1, 0), bq - BF16_ROWS) // BF16_ROWS * BF16_ROWS
        zc = z[row0:, ci * cw:(ci + 1) * cw]
        log_beta = jnp.minimum(zc, 0.0) - jnp.log(1.0 + jnp.exp(-jnp.abs(zc)))
        log_keep = log_beta - zc
        vis = None
        if masked:
            t_idx = lax.broadcasted_iota(jnp.int32, zc.shape, 0)
            s_idx = lax.broadcasted_iota(jnp.int32, zc.shape, 1)
            vis = s_idx < t_idx + (row0 + mask_offset - ci * cw)
            log_keep = jnp.where(vis, log_keep, 0.0)
        hi = log_keep.astype(BF16)
        lo = (log_keep - hi.astype(F32)).astype(BF16)
        chunks.append((log_beta, vis, row0))
        xs.append(jnp.concatenate([hi, lo], axis=1))
    return chunks, xs[0] if nc == 1 else jnp.concatenate(xs, axis=0)


def _sb_weights(chunks, r, c, v):
    cw = chunks[0][0].shape[1]
    ends, row = [], 0
    for log_beta, _, _ in chunks:
        row += log_beta.shape[0]
        ends.append(row)
    ws = [None] * len(chunks)
    for ci in reversed(range(len(chunks))):
        log_beta, vis, row0 = chunks[ci]
        rc = r[ends[ci] - log_beta.shape[0]:ends[ci]]
        w = jnp.exp(log_beta + rc[:, :cw] + c[row0:])
        if vis is not None:
            w = jnp.where(vis, w, 0.0)
        w = w.astype(BF16)
        c_new = c[row0:] + rc[:, cw:]
        if row0:
            w = jnp.concatenate([jnp.zeros((row0, cw), BF16), w], axis=0)
            c_new = jnp.concatenate([c[:row0], c_new], axis=0)
        ws[ci] = w
        c = c_new
    w = ws[0] if len(ws) == 1 else jnp.concatenate(ws, axis=1)
    return jnp.dot(w, v, preferred_element_type=F32), c


def _sb_prompt_kernel(q_ref, k_ref, v_ref, o_ref, acc_scr, c_scr, *, bq, n_par):
    n_blocks = q_ref.shape[0] // bq
    n_groups = (n_blocks - 1) // n_par
    uu = _cumsum_ones(LANES)
    c0 = jnp.zeros((bq, LANES), F32)

    acc, _ = _sb_tile(q_ref[0:bq, :], k_ref[0:bq, :], v_ref[0:bq, :], c0, uu, 0)
    o_ref[0:bq, :] = acc.astype(o_ref.dtype)

    def group(g, carry):
        items, q0s = [], []
        for u in range(n_par):
            q0 = pl.multiple_of((1 + g + u * n_groups) * bq, bq)
            k0 = pl.multiple_of(q0 - bq, bq)
            items.append((q_ref[pl.ds(q0, bq), :], k_ref[pl.ds(k0, 2 * bq), :], v_ref[pl.ds(k0, 2 * bq), :], c0))
            q0s.append(q0)
        res = _sb_tiles(items, uu, bq)
        for q0, (acc, _) in zip(q0s, res):
            o_ref[pl.ds(q0, bq), :] = acc.astype(o_ref.dtype)
        c_max = res[0][1]
        for _, c in res[1:]:
            c_max = jnp.maximum(c_max, c)

        @pl.when(jnp.max(c_max) > LOG_WEIGHT_ZERO)
        def _():
            for u, (acc, c) in enumerate(res):
                acc_scr[u] = acc
                c_scr[u] = c

            def older_keys(u, carry):
                qb = 1 + g + u * n_groups
                q0 = pl.multiple_of(qb * bq, bq)
                k0 = pl.multiple_of(q0 - bq, bq)
                q = q_ref[pl.ds(q0, bq), :]
                n_wide = (qb - 1) // 2

                def more(st):
                    j, _, c = st
                    return jnp.logical_and(j < n_wide, jnp.max(c) > LOG_WEIGHT_ZERO)

                def wide_tile(st):
                    j, acc, c = st
                    ks = pl.multiple_of(k0 - 2 * bq * (j + 1), bq)
                    da, c = _sb_tile(q, k_ref[pl.ds(ks, 2 * bq), :], v_ref[pl.ds(ks, 2 * bq), :], c, uu, None)
                    return j + 1, acc + da, c

                j, acc, c = lax.while_loop(more, wide_tile, (jnp.int32(0), acc_scr[u], c_scr[u]))
                need_last = jnp.logical_and(jnp.logical_and((qb - 1) % 2 == 1, j == n_wide),
                                            jnp.max(c) > LOG_WEIGHT_ZERO)

                def last_tile(acc, c):
                    da, _ = _sb_tile(q, k_ref[0:bq, :], v_ref[0:bq, :], c, uu, None)
                    return acc + da

                acc = lax.cond(need_last, last_tile, lambda acc, c: acc, acc, c)
                o_ref[pl.ds(q0, bq), :] = acc.astype(o_ref.dtype)
                return carry

            lax.fori_loop(0, n_par, older_keys, 0)

        return carry

    lax.fori_loop(0, n_groups, group, 0)


def _sb_prompt(q, k, v):
    h, t, dh = q.shape
    bq = _pick(t, BQ_ATT)
    n_rest = t // bq - 1
    n_par = next(u for u in (7, 4, 3, 2, 1) if n_rest % u == 0)
    spec = pl.BlockSpec((None, t, dh), lambda i: (i, 0, 0))
    return pl.pallas_call(
        functools.partial(_sb_prompt_kernel, bq=bq, n_par=n_par),
        grid=(h,),
        in_specs=[spec, spec, spec],
        out_specs=spec,
        out_shape=jax.ShapeDtypeStruct((h, t, dh), BF16),
        scratch_shapes=[pltpu.VMEM((n_par, bq, dh), F32), pltpu.VMEM((n_par, bq, LANES), F32)],
        compiler_params=_cparams("parallel"),
        name="sb_prompt",
    )(q, k, v)


def _sb_sample_kernel(q_ref, kn_ref, vn_ref, kp_ref, vp_ref, o_ref, acc_scr, c_scr, *, tk):
    n_heads, tq, dh = q_ref.shape
    kt = pl.program_id(1)
    uu = _cumsum_ones(LANES)

    @pl.when(kt == 0)
    def _():
        pad = jnp.zeros((LANES - tq, dh), BF16)
        c0 = jnp.zeros((tq, LANES), F32)
        items = [(q_ref[h], jnp.concatenate([kn_ref[h], pad], axis=0),
                  jnp.concatenate([vn_ref[h], pad], axis=0), c0) for h in range(n_heads)]
        for h, (acc, c) in enumerate(_sb_tiles(items, uu, 0)):
            acc_scr[h] = acc
            c_scr[h] = c

    @pl.when(jnp.max(c_scr[...]) > LOG_WEIGHT_ZERO)
    def _():
        items = [(q_ref[h], kp_ref[0, pl.ds(h, tk, stride=n_heads), :].astype(BF16),
                  vp_ref[0, pl.ds(h, tk, stride=n_heads), :].astype(BF16), c_scr[h]) for h in range(n_heads)]
        for h, (da, c) in enumerate(_sb_tiles(items, uu, None)):
            acc_scr[h] = acc_scr[h] + da
            c_scr[h] = c

    @pl.when(kt == pl.num_programs(1) - 1)
    def _():
        for h in range(n_heads):
            o_ref[h] = acc_scr[h].astype(o_ref.dtype)


def _sb_sample(q, kn, vn, kp, vp, tq):
    h, m, dh = q.shape
    bsz = m // tq
    p = kp.shape[1] // h
    tk = _pick(p, TK_SAMPLE)
    n_kt = p // tk
    assert tq <= LANES and tk % LANES == 0
    new = pl.BlockSpec((h, tq, dh), lambda b, kt: (0, b, 0))
    past = pl.BlockSpec((1, tk * h, dh), lambda b, kt: (b, n_kt - 1 - kt, 0))
    return pl.pallas_call(
        functools.partial(_sb_sample_kernel, tk=tk),
        grid=(bsz, n_kt),
        in_specs=[new, new, new, past, past],
        out_specs=new,
        out_shape=jax.ShapeDtypeStruct((h, m, dh), BF16),
        scratch_shapes=[pltpu.VMEM((h, tq, dh), F32), pltpu.VMEM((h, tq, LANES), F32)],
        compiler_params=_cparams("parallel", "arbitrary"),
        name="sb_sample",
    )(q, kn, vn, kp, vp)


def _shift_rows(x, prev8, s):
    n, c = x.shape
    groups = n // SUBLANES
    rot = pltpu.roll(x.reshape(groups, SUBLANES, c), s, 1)
    before = pltpu.roll(prev8, s, 0).reshape(1, SUBLANES, c)
    if groups > 1:
        before = jnp.concatenate([before, rot[:groups - 1]], axis=0)
    sub = lax.broadcasted_iota(jnp.int32, rot.shape, 1)
    return jnp.where(sub < s, before, rot).reshape(n, c)


def _causal_conv(x, prev8, w, b):
    width = w.shape[0]
    y = b + w[width - 1:width] * x
    for j in range(width - 1):
        y = y + w[j:j + 1] * _shift_rows(x, prev8, width - 1 - j)
    return y


def _linear_scan(a, u, h_in):
    n, c = a.shape
    a = a.reshape(n // SUBLANES, SUBLANES, c)
    u = u.reshape(n // SUBLANES, SUBLANES, c)
    sub = lax.broadcasted_iota(jnp.int32, a.shape, 1)
    d = 1
    while d < SUBLANES:
        keep = sub >= d
        a_sh = jnp.where(keep, pltpu.roll(a, d, 1), 1.0)
        u_sh = jnp.where(keep, pltpu.roll(u, d, 1), 0.0)
        u = u + a * u_sh
        a = a * a_sh
        d *= 2
    a = a.reshape(n, c)
    u = u.reshape(n, c)
    hs = []
    for g in range(n // SUBLANES):
        rows = slice(g * SUBLANES, (g + 1) * SUBLANES)
        hg = a[rows] * h_in + u[rows]
        hs.append(hg)
        h_in = hg[SUBLANES - 1:SUBLANES, :]
    return hs[0] if len(hs) == 1 else jnp.concatenate(hs, axis=0)


def _lru_kernel(xl_ref, gl_ref, prev_ref, h0_ref, cw_ref, cb_ref, wa_ref, ba_ref, wx_ref, bx_ref, lam_ref,
                o_ref, conv8_ref, h8_ref, prev_scr, h_scr, *, n_seq, piece):
    t = pl.program_id(1)

    @pl.when(t == 0)
    def _():
        prev_scr[...] = prev_ref[...]
        h_scr[...] = h0_ref[...]

    lam = lam_ref[...]
    softplus_neg_lam = jnp.maximum(-lam, 0.0) + jnp.log1p(jnp.exp(-jnp.abs(lam)))
    wa = wa_ref[0]
    wx = wx_ref[0]
    seq_rows = xl_ref.shape[0] // n_seq
    for s in range(n_seq):
        prev8 = prev_scr[s]
        h8 = h_scr[s]
        for pc in range(seq_rows // piece):
            rows = slice(s * seq_rows + pc * piece, s * seq_rows + (pc + 1) * piece)
            x = xl_ref[rows, :]
            xc = _causal_conv(x, prev8, cw_ref[...], cb_ref[...])
            xcb = xc.astype(BF16)
            r = jax.nn.sigmoid(jnp.dot(xcb, wa, preferred_element_type=F32) + ba_ref[...])
            i = jax.nn.sigmoid(jnp.dot(xcb, wx, preferred_element_type=F32) + bx_ref[...])
            log_a = -LRU_C * r * softplus_neg_lam
            a = jnp.exp(log_a)
            u = jnp.sqrt(-jnp.tanh(log_a) * (a * a + 1.0)) * (i * xc)
            hs = _linear_scan(a, u, h8[SUBLANES - 1:SUBLANES, :])
            o_ref[rows, :] = (hs * _gelu_tanh(gl_ref[rows, :])).astype(o_ref.dtype)
            prev8 = x[piece - SUBLANES:, :]
            h8 = hs[piece - SUBLANES:, :]
        prev_scr[s] = prev8
        h_scr[s] = h8
        conv8_ref[s] = prev8
        h8_ref[s] = h8


def _lru(xg, prev8, h08, conv_w, conv_b, w_a, b_a, w_x, b_x, lam, n_seq, seq_len):
    m, c2 = xg.shape
    c = c2 // 2
    nb = c // LANES
    assert w_a.shape == (nb, LANES, LANES) and seq_len % SUBLANES == 0 and seq_len >= conv_w.shape[0] - 1
    if n_seq == 1:
        tt = _pick(seq_len, TT_LRU)
        s_t, piece = 1, _pick(tt, PC_LRU)
    else:
        tt = m
        s_t, piece = n_seq, seq_len
    nt = m // tt
    width = conv_w.shape[0]
    vec = pl.BlockSpec((1, LANES), lambda n, t: (0, n))
    st8 = pl.BlockSpec((s_t, SUBLANES, LANES), lambda n, t: (0, 0, n))
    out8 = pl.BlockSpec((s_t, SUBLANES, LANES), lambda n, t: (t, 0, n))
    gate_w = pl.BlockSpec((1, LANES, LANES), lambda n, t: (n, 0, 0))
    return pl.pallas_call(
        functools.partial(_lru_kernel, n_seq=s_t, piece=piece),
        grid=(nb, nt),
        in_specs=[pl.BlockSpec((tt, LANES), lambda n, t: (t, n)),
                  pl.BlockSpec((tt, LANES), lambda n, t: (t, n + nb)),
                  st8, st8,
                  pl.BlockSpec((width, LANES), lambda n, t: (0, n)), vec,
                  gate_w, vec, gate_w, vec, vec],
        out_specs=[pl.BlockSpec((tt, LANES), lambda n, t: (t, n)), out8, out8],
        out_shape=[jax.ShapeDtypeStruct((m, c), BF16),
                   jax.ShapeDtypeStruct((nt * s_t, SUBLANES, c), F32),
                   jax.ShapeDtypeStruct((nt * s_t, SUBLANES, c), F32)],
        scratch_shapes=[pltpu.VMEM((s_t, SUBLANES, LANES), F32), pltpu.VMEM((s_t, SUBLANES, LANES), F32)],
        compiler_params=_cparams("parallel", "arbitrary"),
        name="rg_lru",
    )(xg, xg, prev8, h08, conv_w, conv_b, w_a, b_a, w_x, b_x, lam)


def _ffn_up_kernel(h_ref, wg_ref, wv_ref, pg_ref, pv_ref, cwg_ref, cwv_ref, cbg_ref, cbv_ref,
                   act_ref, lastg_ref, lastv_ref, raw_g, raw_v, carry_g, carry_v, *, n_seq, nj):
    s = pl.program_id(0)
    tile_e = jnp.maximum(s - 1, 0)
    i_e = lax.div(tile_e, nj)
    j_e = lax.rem(tile_e, nj)

    @pl.when(s == 0)
    def _():
        raw_g[1] = jnp.zeros(raw_g.shape[1:], F32)
        raw_v[1] = jnp.zeros(raw_v.shape[1:], F32)

    @pl.when(i_e == 0)
    def _():
        carry_g[j_e] = pg_ref[...]
        carry_v[j_e] = pv_ref[...]

    valid = s > 0
    tm = h_ref.shape[0]
    seq_rows = tm // n_seq
    piece = min(seq_rows, PIECE_FFN)

    def step(cur, prv):
        pieces = [(sq, p0) for sq in range(n_seq) for p0 in range(sq * seq_rows, (sq + 1) * seq_rows, piece)]
        n_k = max(1, min(KCHUNKS_FFN, len(pieces), h_ref.shape[1] // MXU_DEPTH))
        kc_size = h_ref.shape[1] // n_k
        prev_g = [carry_g[j_e, sq] for sq in range(n_seq)]
        prev_v = [carry_v[j_e, sq] for sq in range(n_seq)]
        old_g, old_v = list(prev_g), list(prev_v)
        acc_g = acc_v = None
        done = 0
        for kc in range(n_k):
            ks = slice(kc * kc_size, (kc + 1) * kc_size)
            h = h_ref[:, ks]
            part_g = jnp.dot(h, wg_ref[ks, :].astype(BF16), preferred_element_type=F32)
            part_v = jnp.dot(h, wv_ref[ks, :].astype(BF16), preferred_element_type=F32)
            acc_g = part_g if acc_g is None else acc_g + part_g
            acc_v = part_v if acc_v is None else acc_v + part_v
            upto = len(pieces) * (kc + 1) // n_k
            for sq, p0 in pieces[done:upto]:
                xg = raw_g[prv, p0:p0 + piece, :]
                xv = raw_v[prv, p0:p0 + piece, :]
                yg = _causal_conv(xg, prev_g[sq], cwg_ref[...], cbg_ref[...])
                yv = _causal_conv(xv, prev_v[sq], cwv_ref[...], cbv_ref[...])
                act_ref[p0:p0 + piece, :] = (_gelu_tanh(yg) * yv).astype(act_ref.dtype)
                prev_g[sq] = xg[piece - SUBLANES:, :]
                prev_v[sq] = xv[piece - SUBLANES:, :]
            done = upto
        raw_g[cur] = acc_g
        raw_v[cur] = acc_v
        for sq in range(n_seq):
            carry_g[j_e, sq] = jnp.where(valid, prev_g[sq], old_g[sq])
            carry_v[j_e, sq] = jnp.where(valid, prev_v[sq], old_v[sq])
            lastg_ref[sq] = prev_g[sq]
            lastv_ref[sq] = prev_v[sq]

    slot = lax.rem(s, 2)
    pl.when(slot == 0)(lambda: step(0, 1))
    pl.when(slot == 1)(lambda: step(1, 0))


def _ffn_up(h, w_up, prev8, conv_w, conv_b, n_seq, seq_len):
    m, d = h.shape
    f = w_up.shape[1] // 2
    tn = _pick(f, TN_FFN)
    assert seq_len % SUBLANES == 0 and seq_len >= conv_w.shape[0] - 1
    if n_seq == 1:
        tm = _pick(seq_len, TM_FFN)
        s_t = 1
    else:
        tm = m
        s_t = n_seq
    ni, nj = m // tm, f // tn
    n_tiles = ni * nj
    width = conv_w.shape[0]
    i_d = lambda s: jnp.minimum(s, n_tiles - 1) // nj
    j_d = lambda s: jnp.minimum(s, n_tiles - 1) % nj
    i_e = lambda s: jnp.maximum(s - 1, 0) // nj
    j_e = lambda s: jnp.maximum(s - 1, 0) % nj
    st8 = lambda off: pl.BlockSpec((s_t, SUBLANES, tn), lambda s: (0, 0, j_e(s) + off))
    out8 = pl.BlockSpec((s_t, SUBLANES, tn), lambda s: (i_e(s), 0, j_e(s)))
    wspec = lambda off: pl.BlockSpec((d, tn), lambda s: (0, j_d(s) + off))
    cw = lambda off: pl.BlockSpec((width, tn), lambda s: (0, j_e(s) + off))
    cb = lambda off: pl.BlockSpec((1, tn), lambda s: (0, j_e(s) + off))
    return pl.pallas_call(
        functools.partial(_ffn_up_kernel, n_seq=s_t, nj=nj),
        grid=(n_tiles + 1,),
        in_specs=[pl.BlockSpec((tm, d), lambda s: (i_d(s), 0)),
                  wspec(0), wspec(nj), st8(0), st8(nj), cw(0), cw(nj), cb(0), cb(nj)],
        out_specs=[pl.BlockSpec((tm, tn), lambda s: (i_e(s), j_e(s))), out8, out8],
        out_shape=[jax.ShapeDtypeStruct((m, f), BF16),
                   jax.ShapeDtypeStruct((ni * s_t, SUBLANES, f), F32),
                   jax.ShapeDtypeStruct((ni * s_t, SUBLANES, f), F32)],
        scratch_shapes=[pltpu.VMEM((2, tm, tn), F32), pltpu.VMEM((2, tm, tn), F32),
                        pltpu.VMEM((nj, s_t, SUBLANES, tn), F32), pltpu.VMEM((nj, s_t, SUBLANES, tn), F32)],
        compiler_params=_cparams("arbitrary"),
        name="ffn_up",
    )(h, w_up, w_up, prev8, prev8, conv_w, conv_w, conv_b, conv_b)


def _state8(state, n_rows):
    return jnp.pad(state.astype(F32), ((0, 0), (SUBLANES - n_rows, 0), (0, 0)))


def _layer(x, mods, p, past, n_seq, seq_len):
    m, d = x.shape
    d_sb = d // 2
    d_lru = d // 2
    sh_m, sc_m, g_m, sh_f, sc_f, g_f = mods
    k_past, v_past, conv_lru_prev, h_prev, conv_ffn_prev = past

    h = _norm_mod(x, p['g_pre_mix'], sc_m, sh_m)
    (q,) = _matmul(h, p['w_in'], 0, d_sb, [(BF16, True)], name="proj_q")
    k32, k16 = _matmul(h, p['w_in'], d_sb, d_sb, [(F32, False), (BF16, True)], name="proj_k")
    v32, v16 = _matmul(h, p['w_in'], 2 * d_sb, d_sb, [(F32, False), (BF16, True)], name="proj_v")
    (xg,) = _matmul(h, p['w_in'], 3 * d_sb, 2 * d_lru, [(F32, False)], name="proj_lru")

    if k_past is None:
        o_sb = _sb_prompt(q, k16, v16)
    else:
        o_sb = _sb_sample(q, k16, v16, k_past, v_past, seq_len)

    w_c = p['w_conv_lru']
    o_lru, conv8, h8 = _lru(xg, _state8(conv_lru_prev, w_c.shape[0] - 1),
                            jnp.broadcast_to(h_prev.astype(F32)[:, None, :], (n_seq, SUBLANES, d_lru)),
                            w_c, p['b_conv_lru'], p['w_gate_a'], p['b_gate_a'], p['w_gate_x'], p['b_gate_x'],
                            p['lru_lambda'], n_seq, seq_len)
    conv_lru_new = conv8[-n_seq:, SUBLANES - (w_c.shape[0] - 1):, :]
    h_last = h8[-n_seq:, SUBLANES - 1, :]

    mix_in = _group_norm_concat(o_sb, o_lru, p['g_grp_attn'], p['g_grp_lru'])
    (mixed,) = _matmul(mix_in, p['w_out'], 0, d, [(BF16, False)], name="proj_out")
    x1, h2 = _post_mix(x, mixed, p['g_post_mix'], g_m, p['g_pre_ffn'], sc_f, sh_f)

    w_f = p['w_conv_ffn']
    act, last_g, last_v = _ffn_up(h2, p['w_up'], _state8(conv_ffn_prev, w_f.shape[0] - 1),
                                  w_f, p['b_conv_ffn'], n_seq, seq_len)
    keep = SUBLANES - (w_f.shape[0] - 1)
    conv_ffn_new = jnp.concatenate([last_g[-n_seq:, keep:, :], last_v[-n_seq:, keep:, :]], axis=-1)
    out = _ffn_down_residual(act, p['w_down'], x1, p['g_post_ffn'], g_f)
    return out, (k32, v32, conv_lru_new, h_last, conv_ffn_new)


def kernel(x_prompt, x_sample, c_prompt, c_sample, cache_k, cache_v, state_conv_lru, state_lru, state_conv_ffn, w_ada, b_ada, g_pre_mix, w_in, w_conv_lru, b_conv_lru, w_gate_a, b_gate_a, w_gate_x, b_gate_x, lru_lambda, g_grp_attn, g_grp_lru, w_out, g_post_mix, g_pre_ffn, w_up, w_conv_ffn, b_conv_ffn, w_down, g_post_ffn):
    depth = w_ada.shape[0]
    bp, tp, d = x_prompt.shape
    bs, ts, _ = x_sample.shape
    assert bp == 1, "the prompt kernels treat the prompt as one sequence"
    d_sb = d // 2
    n_heads = d_sb // HEAD_DIM
    n_mod = w_ada.shape[2] // d

    xp = x_prompt.reshape(bp * tp, d)
    xs = x_sample.reshape(bs * ts, d)
    n_c = bp + bs
    n_c_pad = -(-n_c // SUBLANES) * SUBLANES
    c_all = jnp.concatenate([c_prompt, c_sample, jnp.zeros((n_c_pad - n_c, d), F32)], axis=0)

    st_p = [[], [], [], [], []]
    st_s = [[], [], [], [], []]
    for l in range(depth):
        row = lambda a: a[l][None, :]
        p = {
            'g_pre_mix': row(g_pre_mix), 'w_in': w_in[l].astype(BF16),
            'w_conv_lru': w_conv_lru[l], 'b_conv_lru': row(b_conv_lru),
            'w_gate_a': w_gate_a[l].astype(BF16), 'b_gate_a': row(b_gate_a),
            'w_gate_x': w_gate_x[l].astype(BF16), 'b_gate_x': row(b_gate_x),
            'lru_lambda': row(lru_lambda), 'g_grp_attn': row(g_grp_attn), 'g_grp_lru': row(g_grp_lru),
            'w_out': w_out[l].astype(BF16), 'g_post_mix': row(g_post_mix), 'g_pre_ffn': row(g_pre_ffn),
            'w_up': w_up[l], 'w_conv_ffn': w_conv_ffn[l], 'b_conv_ffn': row(b_conv_ffn),
            'w_down': w_down[l].astype(BF16), 'g_post_ffn': row(g_post_ffn),
        }
        mod = _ada(c_all, w_ada[l], row(b_ada))
        mods_p = [mod[0:1, i * d:(i + 1) * d] for i in range(n_mod)]
        mods_s = [jnp.repeat(mod[bp:bp + bs, i * d:(i + 1) * d], ts, axis=0) for i in range(n_mod)]

        d_lru = d // 2
        f2 = w_up.shape[2]
        prompt_past = (None, None, jnp.zeros((bp, w_conv_lru.shape[1] - 1, d_lru), F32),
                       jnp.zeros((bp, d_lru), F32), jnp.zeros((bp, w_conv_ffn.shape[1] - 1, f2), F32))
        xp, new_p = _layer(xp, mods_p, p, prompt_past, bp, tp)
        past_len = cache_k.shape[2]
        sample_past = (cache_k[l].reshape(bs, past_len * n_heads, HEAD_DIM),
                       cache_v[l].reshape(bs, past_len * n_heads, HEAD_DIM),
                       state_conv_lru[l], state_lru[l], state_conv_ffn[l])
        xs, new_s = _layer(xs, mods_s, p, sample_past, bs, ts)
        for j in range(5):
            st_p[j].append(new_p[j])
            st_s[j].append(new_s[j])

    def pack(st, b, t):
        k = jnp.stack(st[0], axis=0).reshape(depth, b, t, n_heads, HEAD_DIM)
        v = jnp.stack(st[1], axis=0).reshape(depth, b, t, n_heads, HEAD_DIM)
        return k, v, jnp.stack(st[2], axis=0), jnp.stack(st[3], axis=0), jnp.stack(st[4], axis=0)

    k_p, v_p, cl_p, h_p, cf_p = pack(st_p, bp, tp)
    k_s, v_s, cl_s, h_s, cf_s = pack(st_s, bs, ts)
    return (xp.reshape(bp, tp, d), xs.reshape(bs, ts, d), k_p, v_p, cl_p, h_p, cf_p, k_s, v_s, cl_s, h_s, cf_s)
```

```python
import functools
import math

import jax
import jax.numpy as jnp
from jax import lax
from jax.experimental import pallas as pl
from jax.experimental.pallas import tpu as pltpu

F32 = jnp.float32
BF16 = jnp.bfloat16

EPS = 1e-6
LRU_C = 8.0
HEAD_DIM = 128
LANES = 128
SUBLANES = 8
MXU_DEPTH = 256
BF16_ROWS = 16
VMEM_LIMIT = 56 * 1024 * 1024
LOG_WEIGHT_ZERO = -110.0

TM_ROW = 256
TM_NORM = 512
TM_MM = 1024
TN_MM = 1024
TN_ADA = 512
TM_FFN = 1024
TN_FFN = 256
PIECE_FFN = 64
KCHUNKS_FFN = 16
TM_DOWN = 512
TN_DOWN = 512
KCHUNKS_DOWN = 4
TT_LRU = 1024
PC_LRU = 256
BQ_ATT = 256
TK_SAMPLE = 512


def _pick(n, pref):
    if n <= pref:
        return n
    t = pref
    while n % t:
        t //= 2
    assert t >= SUBLANES, (n, pref)
    return t


def _cparams(*sem):
    return pltpu.CompilerParams(dimension_semantics=sem, vmem_limit_bytes=VMEM_LIMIT)


def _gelu_tanh(x):
    return 0.5 * x * (1.0 + jnp.tanh(math.sqrt(2.0 / math.pi) * (x + 0.044715 * (x * x * x))))


def _rms(x, g):
    ms = jnp.mean(x * x, axis=-1, keepdims=True)
    return x * lax.rsqrt(ms + EPS) * g


def _row_tile(m, pref, *mods):
    per_row = any(mod.shape[0] != 1 for mod in mods)
    return _pick(m, pref // 2 if per_row else pref)


def _mod_spec(mod, tm, d):
    if mod.shape[0] == 1:
        return pl.BlockSpec((1, d), lambda i: (0, 0))
    return pl.BlockSpec((tm, d), lambda i: (i, 0))


def _ada_kernel(c_ref, w_ref, b_ref, o_ref):
    c = c_ref[...]
    a = (c * jax.nn.sigmoid(c)).astype(BF16)
    o_ref[...] = jnp.dot(a, w_ref[...].astype(BF16), preferred_element_type=F32) + b_ref[...]


def _ada(c, w, b):
    bsz, d = c.shape
    n = w.shape[1]
    tn = _pick(n, TN_ADA)
    return pl.pallas_call(
        _ada_kernel,
        grid=(n // tn,),
        in_specs=[pl.BlockSpec((bsz, d), lambda j: (0, 0)),
                  pl.BlockSpec((d, tn), lambda j: (0, j)),
                  pl.BlockSpec((1, tn), lambda j: (0, j))],
        out_specs=pl.BlockSpec((bsz, tn), lambda j: (0, j)),
        out_shape=jax.ShapeDtypeStruct((bsz, n), F32),
        compiler_params=_cparams("parallel"),
        name="ada_proj",
    )(c, w, b)


def _norm_mod_kernel(x_ref, g_ref, sc_ref, sh_ref, o_ref):
    y = _rms(x_ref[...], g_ref[...])
    o_ref[...] = (y * (1.0 + sc_ref[...]) + sh_ref[...]).astype(o_ref.dtype)


def _norm_mod(x, g, sc, sh):
    m, d = x.shape
    tm = _row_tile(m, TM_NORM, sc, sh)
    return pl.pallas_call(
        _norm_mod_kernel,
        grid=(m // tm,),
        in_specs=[pl.BlockSpec((tm, d), lambda i: (i, 0)),
                  pl.BlockSpec((1, d), lambda i: (0, 0)),
                  _mod_spec(sc, tm, d), _mod_spec(sh, tm, d)],
        out_specs=pl.BlockSpec((tm, d), lambda i: (i, 0)),
        out_shape=jax.ShapeDtypeStruct((m, d), BF16),
        compiler_params=_cparams("parallel"),
        name="norm_mod",
    )(x, g, sc, sh)


def _group_norm_kernel(a_ref, b_ref, ga_ref, gb_ref, o_ref):
    a = jnp.concatenate([a_ref[h] for h in range(a_ref.shape[0])], axis=1).astype(F32)
    da = a.shape[1]
    o_ref[:, :da] = _rms(a, ga_ref[...]).astype(o_ref.dtype)
    o_ref[:, da:] = _rms(b_ref[...].astype(F32), gb_ref[...]).astype(o_ref.dtype)


def _group_norm_concat(a, b, ga, gb):
    h, m, dh = a.shape
    da = h * dh
    db = b.shape[1]
    tm = _pick(m, TM_NORM)
    return pl.pallas_call(
        _group_norm_kernel,
        grid=(m // tm,),
        in_specs=[pl.BlockSpec((h, tm, dh), lambda i: (0, i, 0)),
                  pl.BlockSpec((tm, db), lambda i: (i, 0)),
                  pl.BlockSpec((1, da), lambda i: (0, 0)),
                  pl.BlockSpec((1, db), lambda i: (0, 0))],
        out_specs=pl.BlockSpec((tm, da + db), lambda i: (i, 0)),
        out_shape=jax.ShapeDtypeStruct((m, da + db), BF16),
        compiler_params=_cparams("parallel"),
        name="group_norm_concat",
    )(a, b, ga, gb)


def _post_mix_kernel(x_ref, y_ref, gpost_ref, gate_ref, gpre_ref, sc_ref, sh_ref, x1_ref, h_ref):
    x1 = x_ref[...] + gate_ref[...] * _rms(y_ref[...].astype(F32), gpost_ref[...])
    x1_ref[...] = x1
    h = _rms(x1, gpre_ref[...])
    h_ref[...] = (h * (1.0 + sc_ref[...]) + sh_ref[...]).astype(h_ref.dtype)


def _post_mix(x, y, gpost, gate, gpre, sc, sh):
    m, d = x.shape
    tm = _row_tile(m, TM_ROW, gate, sc, sh)
    row = pl.BlockSpec((tm, d), lambda i: (i, 0))
    vec = pl.BlockSpec((1, d), lambda i: (0, 0))
    return pl.pallas_call(
        _post_mix_kernel,
        grid=(m // tm,),
        in_specs=[row, row, vec, _mod_spec(gate, tm, d), vec, _mod_spec(sc, tm, d), _mod_spec(sh, tm, d)],
        out_specs=[row, row],
        out_shape=[jax.ShapeDtypeStruct((m, d), F32), jax.ShapeDtypeStruct((m, d), BF16)],
        compiler_params=_cparams("parallel"),
        name="post_mix",
    )(x, y, gpost, gate, gpre, sc, sh)


def _ffn_down_kernel(a_ref, b_ref, x_ref, gpost_ref, gate_ref, o_ref, y_scr, ssq_scr, *, nj, ni):
    s = pl.program_id(0)
    row = lax.div(s, nj)
    j = lax.rem(s, nj)
    cur = lax.rem(row, 2)
    tm, tn = o_ref.shape

    @pl.when(s == 0)
    def _():
        y_scr[...] = jnp.zeros(y_scr.shape, y_scr.dtype)
        ssq_scr[...] = jnp.zeros(ssq_scr.shape, F32)

    def residual(rows):
        ms = ssq_scr[1 - cur, rows, 0:1] * (1.0 / (nj * tn))
        gate = gate_ref[rows, :] if gate_ref.shape[0] > 1 else gate_ref[...]
        y_prev = y_scr[j, rows, :].astype(F32)
        o_ref[rows, :] = x_ref[rows, :] + gate * (y_prev * lax.rsqrt(ms + EPS) * gpost_ref[...])

    k = a_ref.shape[1]
    n_k = 1
    while 2 * n_k <= min(KCHUNKS_DOWN, k // MXU_DEPTH) and tm % (2 * n_k * SUBLANES) == 0:
        n_k *= 2
    bounds = [k // MXU_DEPTH * c // n_k * MXU_DEPTH for c in range(n_k)] + [k]
    rows_per = tm // n_k

    @pl.when(row < ni)
    def _():
        y = None
        for c in range(n_k):
            ks = slice(bounds[c], bounds[c + 1])
            part = jnp.dot(a_ref[:, ks], b_ref[ks, :], preferred_element_type=F32)
            y = part if y is None else y + part
            residual(slice(c * rows_per, (c + 1) * rows_per))
        y_scr[j] = y.astype(y_scr.dtype)
        sq = jnp.broadcast_to(jnp.sum(y * y, axis=1, keepdims=True), (tm, LANES))
        ssq_scr[cur] = jnp.where(j == 0, sq, ssq_scr[cur] + sq)

    @pl.when(row == ni)
    def _():
        residual(slice(0, tm))


def _ffn_down_residual(a, b, x, gpost, gate):
    m, k = a.shape
    n = b.shape[1]
    tm = _pick(m, TM_DOWN)
    tn = _pick(n, TN_DOWN)
    ni, nj = m // tm, n // tn
    r_d = lambda s: jnp.minimum(s // nj, ni - 1)
    r_e = lambda s: jnp.maximum(s // nj - 1, 0)
    j_e = lambda s: jnp.where(s < nj, 0, s % nj)
    if gate.shape[0] == 1:
        gate_spec = pl.BlockSpec((1, tn), lambda s: (0, j_e(s)))
    else:
        gate_spec = pl.BlockSpec((tm, tn), lambda s: (r_e(s), j_e(s)))
    return pl.pallas_call(
        functools.partial(_ffn_down_kernel, nj=nj, ni=ni),
        grid=((ni + 1) * nj,),
        in_specs=[pl.BlockSpec((tm, k), lambda s: (r_d(s), 0)),
                  pl.BlockSpec((k, tn), lambda s: (0, jnp.where(s // nj < ni, s % nj, nj - 1))),
                  pl.BlockSpec((tm, tn), lambda s: (r_e(s), j_e(s))),
                  pl.BlockSpec((1, tn), lambda s: (0, j_e(s))),
                  gate_spec],
        out_specs=pl.BlockSpec((tm, tn), lambda s: (r_e(s), j_e(s))),
        out_shape=jax.ShapeDtypeStruct((m, n), F32),
        scratch_shapes=[pltpu.VMEM((nj, tm, tn), BF16), pltpu.VMEM((2, tm, LANES), F32)],
        compiler_params=_cparams("arbitrary"),
        name="ffn_down",
    )(a, b, x, gpost, gate)


def _matmul_kernel(a_ref, b_ref, *o_refs):
    acc = jnp.dot(a_ref[...], b_ref[...], preferred_element_type=F32)
    for o_ref in o_refs:
        if len(o_ref.shape) == 3:
            for h in range(o_ref.shape[0]):
                o_ref[h] = acc[:, h * LANES:(h + 1) * LANES].astype(o_ref.dtype)
        else:
            o_ref[...] = acc.astype(o_ref.dtype)


def _matmul(a, b, col0, n, outs, tm_pref=None, tn_pref=None, name="matmul"):
    m, k = a.shape
    tm = _pick(m, TM_MM if tm_pref is None else tm_pref)
    tn = _pick(n, TN_MM if tn_pref is None else tn_pref)
    assert col0 % tn == 0 and tn % LANES == 0
    off = col0 // tn
    hpt = tn // LANES
    out_specs, out_shape = [], []
    for dt, head_major in outs:
        if head_major:
            out_specs.append(pl.BlockSpec((hpt, tm, LANES), lambda i, j: (j, i, 0)))
            out_shape.append(jax.ShapeDtypeStruct((n // LANES, m, LANES), dt))
        else:
            out_specs.append(pl.BlockSpec((tm, tn), lambda i, j: (i, j)))
            out_shape.append(jax.ShapeDtypeStruct((m, n), dt))
    return pl.pallas_call(
        _matmul_kernel,
        grid=(m // tm, n // tn),
        in_specs=[pl.BlockSpec((tm, k), lambda i, j: (i, 0)),
                  pl.BlockSpec((k, tn), lambda i, j: (0, j + off))],
        out_specs=out_specs,
        out_shape=out_shape,
        compiler_params=_cparams("parallel", "arbitrary"),
        name=name,
    )(a, b)


def _cumsum_ones(cw):
    j = lax.broadcasted_iota(jnp.int32, (2 * cw, 2 * cw), 0)
    s = lax.broadcasted_iota(jnp.int32, (2 * cw, 2 * cw), 1)
    jj = jnp.where(j >= cw, j - cw, j)
    return jnp.where((s >= cw) | (jj > s), 1.0, 0.0).astype(BF16)


def _sb_tile(q, k, v, c, uu, mask_offset):
    return _sb_tiles([(q, k, v, c)], uu, mask_offset)[0]


def _sb_tiles(items, uu, mask_offset):
    scored = [_sb_scores(q, k, uu.shape[0] // 2, mask_offset) for q, k, _, _ in items]
    xs = [x for _, x in scored]
    r = jnp.dot(xs[0] if len(xs) == 1 else jnp.concatenate(xs, axis=0), uu, preferred_element_type=F32)
    outs, row = [], 0
    for (_, _, v, c), (chunks, x) in zip(items, scored):
        outs.append(_sb_weights(chunks, r[row:row + x.shape[0]], c, v))
        row += x.shape[0]
    return outs


def _sb_scores(q, k, cw, mask_offset):
    bq, dh = q.shape
    nc = k.shape[0] // cw
    z = lax.dot_general(q, k, (((1,), (1,)), ((), ())), preferred_element_type=F32) * (dh ** -0.5)
    chunks, xs = [], []
    for ci in range(nc):
        masked = mask_offset is not None and (ci + 1) * cw > mask_offset
        row0 = 0
        if masked:
            row0 = min(max(ci * cw - mask_offset + 1, 0), bq - BF16_ROWS) // BF16_ROWS * BF16_ROWS
        zc = z[row0:, ci * cw:(ci + 1) * cw]
        log_beta = jnp.minimum(zc, 0.0) - jnp.log(1.0 + jnp.exp(-jnp.abs(zc)))
        log_keep = log_beta - zc
        vis = None
        if masked:
            t_idx = lax.broadcasted_iota(jnp.int32, zc.shape, 0)
            s_idx = lax.broadcasted_iota(jnp.int32, zc.shape, 1)
            vis = s_idx < t_idx + (row0 + mask_offset - ci * cw)
            log_keep = jnp.where(vis, log_keep, 0.0)
        hi = log_keep.astype(BF16)
        lo = (log_keep - hi.astype(F32)).astype(BF16)
        chunks.append((log_beta, vis, row0))
        xs.append(jnp.concatenate([hi, lo], axis=1))
    return chunks, xs[0] if nc == 1 else jnp.concatenate(xs, axis=0)


def _sb_weights(chunks, r, c, v):
    cw = chunks[0][0].shape[1]
    ends, row = [], 0
    for log_beta, _, _ in chunks:
        row += log_beta.shape[0]
        ends.append(row)
    ws = [None] * len(chunks)
    for ci in reversed(range(len(chunks))):
        log_beta, vis, row0 = chunks[ci]
        rc = r[ends[ci] - log_beta.shape[0]:ends[ci]]
        w = jnp.exp(log_beta + rc[:, :cw] + c[row0:])
        if vis is not None:
            w = jnp.where(vis, w, 0.0)
        w = w.astype(BF16)
        c_new = c[row0:] + rc[:, cw:]
        if row0:
            w = jnp.concatenate([jnp.zeros((row0, cw), BF16), w], axis=0)
            c_new = jnp.concatenate([c[:row0], c_new], axis=0)
        ws[ci] = w
        c = c_new
    w = ws[0] if len(ws) == 1 else jnp.concatenate(ws, axis=1)
    return jnp.dot(w, v, preferred_element_type=F32), c


def _sb_prompt_kernel(q_ref, k_ref, v_ref, o_ref, acc_scr, c_scr, *, bq, n_par):
    n_blocks = q_ref.shape[0] // bq
    n_groups = (n_blocks - 1) // n_par
    uu = _cumsum_ones(LANES)
    c0 = jnp.zeros((bq, LANES), F32)

    acc, _ = _sb_tile(q_ref[0:bq, :], k_ref[0:bq, :], v_ref[0:bq, :], c0, uu, 0)
    o_ref[0:bq, :] = acc.astype(o_ref.dtype)

    def group(g, carry):
        items, q0s = [], []
        for u in range(n_par):
            q0 = pl.multiple_of((1 + g + u * n_groups) * bq, bq)
            k0 = pl.multiple_of(q0 - bq, bq)
            items.append((q_ref[pl.ds(q0, bq), :], k_ref[pl.ds(k0, 2 * bq), :], v_ref[pl.ds(k0, 2 * bq), :], c0))
            q0s.append(q0)
        res = _sb_tiles(items, uu, bq)
        for q0, (acc, _) in zip(q0s, res):
            o_ref[pl.ds(q0, bq), :] = acc.astype(o_ref.dtype)
        c_max = res[0][1]
        for _, c in res[1:]:
            c_max = jnp.maximum(c_max, c)

        @pl.when(jnp.max(c_max) > LOG_WEIGHT_ZERO)
        def _():
            for u, (acc, c) in enumerate(res):
                acc_scr[u] = acc
                c_scr[u] = c

            def older_keys(u, carry):
                qb = 1 + g + u * n_groups
                q0 = pl.multiple_of(qb * bq, bq)
                k0 = pl.multiple_of(q0 - bq, bq)
                q = q_ref[pl.ds(q0, bq), :]
                n_wide = (qb - 1) // 2

                def more(st):
                    j, _, c = st
                    return jnp.logical_and(j < n_wide, jnp.max(c) > LOG_WEIGHT_ZERO)

                def wide_tile(st):
                    j, acc, c = st
                    ks = pl.multiple_of(k0 - 2 * bq * (j + 1), bq)
                    da, c = _sb_tile(q, k_ref[pl.ds(ks, 2 * bq), :], v_ref[pl.ds(ks, 2 * bq), :], c, uu, None)
                    return j + 1, acc + da, c

                j, acc, c = lax.while_loop(more, wide_tile, (jnp.int32(0), acc_scr[u], c_scr[u]))
                need_last = jnp.logical_and(jnp.logical_and((qb - 1) % 2 == 1, j == n_wide),
                                            jnp.max(c) > LOG_WEIGHT_ZERO)

                def last_tile(acc, c):
                    da, _ = _sb_tile(q, k_ref[0:bq, :], v_ref[0:bq, :], c, uu, None)
                    return acc + da

                acc = lax.cond(need_last, last_tile, lambda acc, c: acc, acc, c)
                o_ref[pl.ds(q0, bq), :] = acc.astype(o_ref.dtype)
                return carry

            lax.fori_loop(0, n_par, older_keys, 0)

        return carry

    lax.fori_loop(0, n_groups, group, 0)


def _sb_prompt(q, k, v):
    h, t, dh = q.shape
    bq = _pick(t, BQ_ATT)
    n_rest = t // bq - 1
    n_par = next(u for u in (7, 4, 3, 2, 1) if n_rest % u == 0)
    spec = pl.BlockSpec((None, t, dh), lambda i: (i, 0, 0))
    return pl.pallas_call(
        functools.partial(_sb_prompt_kernel, bq=bq, n_par=n_par),
        grid=(h,),
        in_specs=[spec, spec, spec],
        out_specs=spec,
        out_shape=jax.ShapeDtypeStruct((h, t, dh), BF16),
        scratch_shapes=[pltpu.VMEM((n_par, bq, dh), F32), pltpu.VMEM((n_par, bq, LANES), F32)],
        compiler_params=_cparams("parallel"),
        name="sb_prompt",
    )(q, k, v)


def _sb_sample_kernel(q_ref, kn_ref, vn_ref, kp_ref, vp_ref, o_ref, acc_scr, c_scr, *, tk):
    n_heads, tq, dh = q_ref.shape
    kt = pl.program_id(1)
    uu = _cumsum_ones(LANES)

    @pl.when(kt == 0)
    def _():
        pad = jnp.zeros((LANES - tq, dh), BF16)
        c0 = jnp.zeros((tq, LANES), F32)
        items = [(q_ref[h], jnp.concatenate([kn_ref[h], pad], axis=0),
                  jnp.concatenate([vn_ref[h], pad], axis=0), c0) for h in range(n_heads)]
        for h, (acc, c) in enumerate(_sb_tiles(items, uu, 0)):
            acc_scr[h] = acc
            c_scr[h] = c

    @pl.when(jnp.max(c_scr[...]) > LOG_WEIGHT_ZERO)
    def _():
        items = [(q_ref[h], kp_ref[0, pl.ds(h, tk, stride=n_heads), :].astype(BF16),
                  vp_ref[0, pl.ds(h, tk, stride=n_heads), :].astype(BF16), c_scr[h]) for h in range(n_heads)]
        for h, (da, c) in enumerate(_sb_tiles(items, uu, None)):
            acc_scr[h] = acc_scr[h] + da
            c_scr[h] = c

    @pl.when(kt == pl.num_programs(1) - 1)
    def _():
        for h in range(n_heads):
            o_ref[h] = acc_scr[h].astype(o_ref.dtype)


def _sb_sample(q, kn, vn, kp, vp, tq):
    h, m, dh = q.shape
    bsz = m // tq
    p = kp.shape[1] // h
    tk = _pick(p, TK_SAMPLE)
    n_kt = p // tk
    assert tq <= LANES and tk % LANES == 0
    new = pl.BlockSpec((h, tq, dh), lambda b, kt: (0, b, 0))
    past = pl.BlockSpec((1, tk * h, dh), lambda b, kt: (b, n_kt - 1 - kt, 0))
    return pl.pallas_call(
        functools.partial(_sb_sample_kernel, tk=tk),
        grid=(bsz, n_kt),
        in_specs=[new, new, new, past, past],
        out_specs=new,
        out_shape=jax.ShapeDtypeStruct((h, m, dh), BF16),
        scratch_shapes=[pltpu.VMEM((h, tq, dh), F32), pltpu.VMEM((h, tq, LANES), F32)],
        compiler_params=_cparams("parallel", "arbitrary"),
        name="sb_sample",
    )(q, kn, vn, kp, vp)


def _shift_rows(x, prev8, s):
    n, c = x.shape
    groups = n // SUBLANES
    rot = pltpu.roll(x.reshape(groups, SUBLANES, c), s, 1)
    before = pltpu.roll(prev8, s, 0).reshape(1, SUBLANES, c)
    if groups > 1:
        before = jnp.concatenate([before, rot[:groups - 1]], axis=0)
    sub = lax.broadcasted_iota(jnp.int32, rot.shape, 1)
    return jnp.where(sub < s, before, rot).reshape(n, c)


def _causal_conv(x, prev8, w, b):
    width = w.shape[0]
    y = b + w[width - 1:width] * x
    for j in range(width - 1):
        y = y + w[j:j + 1] * _shift_rows(x, prev8, width - 1 - j)
    return y


def _linear_scan(a, u, h_in):
    n, c = a.shape
    a = a.reshape(n // SUBLANES, SUBLANES, c)
    u = u.reshape(n // SUBLANES, SUBLANES, c)
    sub = lax.broadcasted_iota(jnp.int32, a.shape, 1)
    d = 1
    while d < SUBLANES:
        keep = sub >= d
        a_sh = jnp.where(keep, pltpu.roll(a, d, 1), 1.0)
        u_sh = jnp.where(keep, pltpu.roll(u, d, 1), 0.0)
        u = u + a * u_sh
        a = a * a_sh
        d *= 2
    a = a.reshape(n, c)
    u = u.reshape(n, c)
    hs = []
    for g in range(n // SUBLANES):
        rows = slice(g * SUBLANES, (g + 1) * SUBLANES)
        hg = a[rows] * h_in + u[rows]
        hs.append(hg)
        h_in = hg[SUBLANES - 1:SUBLANES, :]
    return hs[0] if len(hs) == 1 else jnp.concatenate(hs, axis=0)


def _lru_kernel(xl_ref, gl_ref, prev_ref, h0_ref, cw_ref, cb_ref, wa_ref, ba_ref, wx_ref, bx_ref, lam_ref,
                o_ref, conv8_ref, h8_ref, prev_scr, h_scr, *, n_seq, piece):
    t = pl.program_id(1)

    @pl.when(t == 0)
    def _():
        prev_scr[...] = prev_ref[...]
        h_scr[...] = h0_ref[...]

    lam = lam_ref[...]
    softplus_neg_lam = jnp.maximum(-lam, 0.0) + jnp.log1p(jnp.exp(-jnp.abs(lam)))
    wa = wa_ref[0]
    wx = wx_ref[0]
    seq_rows = xl_ref.shape[0] // n_seq
    for s in range(n_seq):
        prev8 = prev_scr[s]
        h8 = h_scr[s]
        for pc in range(seq_rows // piece):
            rows = slice(s * seq_rows + pc * piece, s * seq_rows + (pc + 1) * piece)
            x = xl_ref[rows, :]
            xc = _causal_conv(x, prev8, cw_ref[...], cb_ref[...])
            xcb = xc.astype(BF16)
            r = jax.nn.sigmoid(jnp.dot(xcb, wa, preferred_element_type=F32) + ba_ref[...])
            i = jax.nn.sigmoid(jnp.dot(xcb, wx, preferred_element_type=F32) + bx_ref[...])
            log_a = -LRU_C * r * softplus_neg_lam
            a = jnp.exp(log_a)
            u = jnp.sqrt(-jnp.tanh(log_a) * (a * a + 1.0)) * (i * xc)
            hs = _linear_scan(a, u, h8[SUBLANES - 1:SUBLANES, :])
            o_ref[rows, :] = (hs * _gelu_tanh(gl_ref[rows, :])).astype(o_ref.dtype)
            prev8 = x[piece - SUBLANES:, :]
            h8 = hs[piece - SUBLANES:, :]
        prev_scr[s] = prev8
        h_scr[s] = h8
        conv8_ref[s] = prev8
        h8_ref[s] = h8


def _lru(xg, prev8, h08, conv_w, conv_b, w_a, b_a, w_x, b_x, lam, n_seq, seq_len):
    m, c2 = xg.shape
    c = c2 // 2
    nb = c // LANES
    assert w_a.shape == (nb, LANES, LANES) and seq_len % SUBLANES == 0 and seq_len >= conv_w.shape[0] - 1
    if n_seq == 1:
        tt = _pick(seq_len, TT_LRU)
        s_t, piece = 1, _pick(tt, PC_LRU)
    else:
        tt = m
        s_t, piece = n_seq, seq_len
    nt = m // tt
    width = conv_w.shape[0]
    vec = pl.BlockSpec((1, LANES), lambda n, t: (0, n))
    st8 = pl.BlockSpec((s_t, SUBLANES, LANES), lambda n, t: (0, 0, n))
    out8 = pl.BlockSpec((s_t, SUBLANES, LANES), lambda n, t: (t, 0, n))
    gate_w = pl.BlockSpec((1, LANES, LANES), lambda n, t: (n, 0, 0))
    return pl.pallas_call(
        functools.partial(_lru_kernel, n_seq=s_t, piece=piece),
        grid=(nb, nt),
        in_specs=[pl.BlockSpec((tt, LANES), lambda n, t: (t, n)),
                  pl.BlockSpec((tt, LANES), lambda n, t: (t, n + nb)),
                  st8, st8,
                  pl.BlockSpec((width, LANES), lambda n, t: (0, n)), vec,
                  gate_w, vec, gate_w, vec, vec],
        out_specs=[pl.BlockSpec((tt, LANES), lambda n, t: (t, n)), out8, out8],
        out_shape=[jax.ShapeDtypeStruct((m, c), BF16),
                   jax.ShapeDtypeStruct((nt * s_t, SUBLANES, c), F32),
                   jax.ShapeDtypeStruct((nt * s_t, SUBLANES, c), F32)],
        scratch_shapes=[pltpu.VMEM((s_t, SUBLANES, LANES), F32), pltpu.VMEM((s_t, SUBLANES, LANES), F32)],
        compiler_params=_cparams("parallel", "arbitrary"),
        name="rg_lru",
    )(xg, xg, prev8, h08, conv_w, conv_b, w_a, b_a, w_x, b_x, lam)


def _ffn_up_kernel(h_ref, wg_ref, wv_ref, pg_ref, pv_ref, cwg_ref, cwv_ref, cbg_ref, cbv_ref,
                   act_ref, lastg_ref, lastv_ref, raw_g, raw_v, carry_g, carry_v, *, n_seq, nj):
    s = pl.program_id(0)
    tile_e = jnp.maximum(s - 1, 0)
    i_e = lax.div(tile_e, nj)
    j_e = lax.rem(tile_e, nj)

    @pl.when(s == 0)
    def _():
        raw_g[1] = jnp.zeros(raw_g.shape[1:], F32)
        raw_v[1] = jnp.zeros(raw_v.shape[1:], F32)

    @pl.when(i_e == 0)
    def _():
        carry_g[j_e] = pg_ref[...]
        carry_v[j_e] = pv_ref[...]

    valid = s > 0
    tm = h_ref.shape[0]
    seq_rows = tm // n_seq
    piece = min(seq_rows, PIECE_FFN)

    def step(cur, prv):
        pieces = [(sq, p0) for sq in range(n_seq) for p0 in range(sq * seq_rows, (sq + 1) * seq_rows, piece)]
        n_k = max(1, min(KCHUNKS_FFN, len(pieces), h_ref.shape[1] // MXU_DEPTH))
        kc_size = h_ref.shape[1] // n_k
        prev_g = [carry_g[j_e, sq] for sq in range(n_seq)]
        prev_v = [carry_v[j_e, sq] for sq in range(n_seq)]
        old_g, old_v = list(prev_g), list(prev_v)
        acc_g = acc_v = None
        done = 0
        for kc in range(n_k):
            ks = slice(kc * kc_size, (kc + 1) * kc_size)
            h = h_ref[:, ks]
            part_g = jnp.dot(h, wg_ref[ks, :].astype(BF16), preferred_element_type=F32)
            part_v = jnp.dot(h, wv_ref[ks, :].astype(BF16), preferred_element_type=F32)
            acc_g = part_g if acc_g is None else acc_g + part_g
            acc_v = part_v if acc_v is None else acc_v + part_v
            upto = len(pieces) * (kc + 1) // n_k
            for sq, p0 in pieces[done:upto]:
                xg = raw_g[prv, p0:p0 + piece, :]
                xv = raw_v[prv, p0:p0 + piece, :]
                yg = _causal_conv(xg, prev_g[sq], cwg_ref[...], cbg_ref[...])
                yv = _causal_conv(xv, prev_v[sq], cwv_ref[...], cbv_ref[...])
                act_ref[p0:p0 + piece, :] = (_gelu_tanh(yg) * yv).astype(act_ref.dtype)
                prev_g[sq] = xg[piece - SUBLANES:, :]
                prev_v[sq] = xv[piece - SUBLANES:, :]
            done = upto
        raw_g[cur] = acc_g
        raw_v[cur] = acc_v
        for sq in range(n_seq):
            carry_g[j_e, sq] = jnp.where(valid, prev_g[sq], old_g[sq])
            carry_v[j_e, sq] = jnp.where(valid, prev_v[sq], old_v[sq])
            lastg_ref[sq] = prev_g[sq]
            lastv_ref[sq] = prev_v[sq]

    slot = lax.rem(s, 2)
    pl.when(slot == 0)(lambda: step(0, 1))
    pl.when(slot == 1)(lambda: step(1, 0))


def _ffn_up(h, w_up, prev8, conv_w, conv_b, n_seq, seq_len):
    m, d = h.shape
    f = w_up.shape[1] // 2
    tn = _pick(f, TN_FFN)
    assert seq_len % SUBLANES == 0 and seq_len >= conv_w.shape[0] - 1
    if n_seq == 1:
        tm = _pick(seq_len, TM_FFN)
        s_t = 1
    else:
        tm = m
        s_t = n_seq
    ni, nj = m // tm, f // tn
    n_tiles = ni * nj
    width = conv_w.shape[0]
    i_d = lambda s: jnp.minimum(s, n_tiles - 1) // nj
    j_d = lambda s: jnp.minimum(s, n_tiles - 1) % nj
    i_e = lambda s: jnp.maximum(s - 1, 0) // nj
    j_e = lambda s: jnp.maximum(s - 1, 0) % nj
    st8 = lambda off: pl.BlockSpec((s_t, SUBLANES, tn), lambda s: (0, 0, j_e(s) + off))
    out8 = pl.BlockSpec((s_t, SUBLANES, tn), lambda s: (i_e(s), 0, j_e(s)))
    wspec = lambda off: pl.BlockSpec((d, tn), lambda s: (0, j_d(s) + off))
    cw = lambda off: pl.BlockSpec((width, tn), lambda s: (0, j_e(s) + off))
    cb = lambda off: pl.BlockSpec((1, tn), lambda s: (0, j_e(s) + off))
    return pl.pallas_call(
        functools.partial(_ffn_up_kernel, n_seq=s_t, nj=nj),
        grid=(n_tiles + 1,),
        in_specs=[pl.BlockSpec((tm, d), lambda s: (i_d(s), 0)),
                  wspec(0), wspec(nj), st8(0), st8(nj), cw(0), cw(nj), cb(0), cb(nj)],
        out_specs=[pl.BlockSpec((tm, tn), lambda s: (i_e(s), j_e(s))), out8, out8],
        out_shape=[jax.ShapeDtypeStruct((m, f), BF16),
                   jax.ShapeDtypeStruct((ni * s_t, SUBLANES, f), F32),
                   jax.ShapeDtypeStruct((ni * s_t, SUBLANES, f), F32)],
        scratch_shapes=[pltpu.VMEM((2, tm, tn), F32), pltpu.VMEM((2, tm, tn), F32),
                        pltpu.VMEM((nj, s_t, SUBLANES, tn), F32), pltpu.VMEM((nj, s_t, SUBLANES, tn), F32)],
        compiler_params=_cparams("arbitrary"),
        name="ffn_up",
    )(h, w_up, w_up, prev8, prev8, conv_w, conv_w, conv_b, conv_b)


def _state8(state, n_rows):
    return jnp.pad(state.astype(F32), ((0, 0), (SUBLANES - n_rows, 0), (0, 0)))


def _layer(x, mods, p, past, n_seq, seq_len):
    m, d = x.shape
    d_sb = d // 2
    d_lru = d // 2
    sh_m, sc_m, g_m, sh_f, sc_f, g_f = mods
    k_past, v_past, conv_lru_prev, h_prev, conv_ffn_prev = past

    h = _norm_mod(x, p['g_pre_mix'], sc_m, sh_m)
    (q,) = _matmul(h, p['w_in'], 0, d_sb, [(BF16, True)], name="proj_q")
    k32, k16 = _matmul(h, p['w_in'], d_sb, d_sb, [(F32, False), (BF16, True)], name="proj_k")
    v32, v16 = _matmul(h, p['w_in'], 2 * d_sb, d_sb, [(F32, False), (BF16, True)], name="proj_v")
    (xg,) = _matmul(h, p['w_in'], 3 * d_sb, 2 * d_lru, [(F32, False)], name="proj_lru")

    if k_past is None:
        o_sb = _sb_prompt(q, k16, v16)
    else:
        o_sb = _sb_sample(q, k16, v16, k_past, v_past, seq_len)

    w_c = p['w_conv_lru']
    o_lru, conv8, h8 = _lru(xg, _state8(conv_lru_prev, w_c.shape[0] - 1),
                            jnp.broadcast_to(h_prev.astype(F32)[:, None, :], (n_seq, SUBLANES, d_lru)),
                            w_c, p['b_conv_lru'], p['w_gate_a'], p['b_gate_a'], p['w_gate_x'], p['b_gate_x'],
                            p['lru_lambda'], n_seq, seq_len)
    conv_lru_new = conv8[-n_seq:, SUBLANES - (w_c.shape[0] - 1):, :]
    h_last = h8[-n_seq:, SUBLANES - 1, :]

    mix_in = _group_norm_concat(o_sb, o_lru, p['g_grp_attn'], p['g_grp_lru'])
    (mixed,) = _matmul(mix_in, p['w_out'], 0, d, [(BF16, False)], name="proj_out")
    x1, h2 = _post_mix(x, mixed, p['g_post_mix'], g_m, p['g_pre_ffn'], sc_f, sh_f)

    w_f = p['w_conv_ffn']
    act, last_g, last_v = _ffn_up(h2, p['w_up'], _state8(conv_ffn_prev, w_f.shape[0] - 1),
                                  w_f, p['b_conv_ffn'], n_seq, seq_len)
    keep = SUBLANES - (w_f.shape[0] - 1)
    conv_ffn_new = jnp.concatenate([last_g[-n_seq:, keep:, :], last_v[-n_seq:, keep:, :]], axis=-1)
    out = _ffn_down_residual(act, p['w_down'], x1, p['g_post_ffn'], g_f)
    return out, (k32, v32, conv_lru_new, h_last, conv_ffn_new)


def kernel(x_prompt, x_sample, c_prompt, c_sample, cache_k, cache_v, state_conv_lru, state_lru, state_conv_ffn, w_ada, b_ada, g_pre_mix, w_in, w_conv_lru, b_conv_lru, w_gate_a, b_gate_a, w_gate_x, b_gate_x, lru_lambda, g_grp_attn, g_grp_lru, w_out, g_post_mix, g_pre_ffn, w_up, w_conv_ffn, b_conv_ffn, w_down, g_post_ffn):
    depth = w_ada.shape[0]
    bp, tp, d = x_prompt.shape
    bs, ts, _ = x_sample.shape
    assert bp == 1, "the prompt kernels treat the prompt as one sequence"
    d_sb = d // 2
    n_heads = d_sb // HEAD_DIM
    n_mod = w_ada.shape[2] // d

    xp = x_prompt.reshape(bp * tp, d)
    xs = x_sample.reshape(bs * ts, d)
    n_c = bp + bs
    n_c_pad = -(-n_c // SUBLANES) * SUBLANES
    c_all = jnp.concatenate([c_prompt, c_sample, jnp.zeros((n_c_pad - n_c, d), F32)], axis=0)

    st_p = [[], [], [], [], []]
    st_s = [[], [], [], [], []]
    for l in range(depth):
        row = lambda a: a[l][None, :]
        p = {
            'g_pre_mix': row(g_pre_mix), 'w_in': w_in[l].astype(BF16),
            'w_conv_lru': w_conv_lru[l], 'b_conv_lru': row(b_conv_lru),
            'w_gate_a': w_gate_a[l].astype(BF16), 'b_gate_a': row(b_gate_a),
            'w_gate_x': w_gate_x[l].astype(BF16), 'b_gate_x': row(b_gate_x),
            'lru_lambda': row(lru_lambda), 'g_grp_attn': row(g_grp_attn), 'g_grp_lru': row(g_grp_lru),
            'w_out': w_out[l].astype(BF16), 'g_post_mix': row(g_post_mix), 'g_pre_ffn': row(g_pre_ffn),
            'w_up': w_up[l], 'w_conv_ffn': w_conv_ffn[l], 'b_conv_ffn': row(b_conv_ffn),
            'w_down': w_down[l].astype(BF16), 'g_post_ffn': row(g_post_ffn),
        }
        mod = _ada(c_all, w_ada[l], row(b_ada))
        mods_p = [mod[0:1, i * d:(i + 1) * d] for i in range(n_mod)]
        mods_s = [jnp.repeat(mod[bp:bp + bs, i * d:(i + 1) * d], ts, axis=0) for i in range(n_mod)]

        d_lru = d // 2
        f2 = w_up.shape[2]
        prompt_past = (None, None, jnp.zeros((bp, w_conv_lru.shape[1] - 1, d_lru), F32),
                       jnp.zeros((bp, d_lru), F32), jnp.zeros((bp, w_conv_ffn.shape[1] - 1, f2), F32))
        xp, new_p = _layer(xp, mods_p, p, prompt_past, bp, tp)
        past_len = cache_k.shape[2]
        sample_past = (cache_k[l].reshape(bs, past_len * n_heads, HEAD_DIM),
                       cache_v[l].reshape(bs, past_len * n_heads, HEAD_DIM),
                       state_conv_lru[l], state_lru[l], state_conv_ffn[l])
        xs, new_s = _layer(xs, mods_s, p, sample_past, bs, ts)
        for j in range(5):
            st_p[j].append(new_p[j])
            st_s[j].append(new_s[j])

    def pack(st, b, t):
        k = jnp.stack(st[0], axis=0).reshape(depth, b, t, n_heads, HEAD_DIM)
        v = jnp.stack(st[1], axis=0).reshape(depth, b, t, n_heads, HEAD_DIM)
        return k, v, jnp.stack(st[2], axis=0), jnp.stack(st[3], axis=0), jnp.stack(st[4], axis=0)

    k_p, v_p, cl_p, h_p, cf_p = pack(st_p, bp, tp)
    k_s, v_s, cl_s, h_s, cf_s = pack(st_s, bs, ts)
    return (xp.reshape(bp, tp, d), xs.reshape(bs, ts, d), k_p, v_p, cl_p, h_p, cf_p, k_s, v_s, cl_s, h_s, cf_s)
```

```python
import functools
import math

import jax
import jax.numpy as jnp
from jax import lax
from jax.experimental import pallas as pl
from jax.experimental.pallas import tpu as pltpu

F32 = jnp.float32
BF16 = jnp.bfloat16

EPS = 1e-6
LRU_C = 8.0
HEAD_DIM = 128
LANES = 128
SUBLANES = 8
MXU_DEPTH = 256
BF16_ROWS = 16
VMEM_LIMIT = 56 * 1024 * 1024
LOG_WEIGHT_ZERO = -110.0

TM_ROW = 256
TM_MM = 1024
TN_MM = 1024
TN_ADA = 512
TM_FFN = 1024
TN_FFN = 256
PIECE_FFN = 64
KCHUNKS_FFN = 16
TM_DOWN = 512
TN_DOWN = 512
KCHUNKS_DOWN = 4
TT_LRU = 1024
PC_LRU = 256
BQ_ATT = 256
TK_SAMPLE = 512


def _pick(n, pref):
    if n <= pref:
        return n
    t = pref
    while n % t:
        t //= 2
    assert t >= SUBLANES, (n, pref)
    return t


def _cparams(*sem):
    return pltpu.CompilerParams(dimension_semantics=sem, vmem_limit_bytes=VMEM_LIMIT)


def _gelu_tanh(x):
    return 0.5 * x * (1.0 + jnp.tanh(math.sqrt(2.0 / math.pi) * (x + 0.044715 * (x * x * x))))


def _rms(x, g):
    ms = jnp.mean(x * x, axis=-1, keepdims=True)
    return x * lax.rsqrt(ms + EPS) * g


def _row_tile(m, *mods):
    per_row = any(mod.shape[0] != 1 for mod in mods)
    return _pick(m, TM_ROW // 2 if per_row else TM_ROW)


def _mod_spec(mod, tm, d):
    if mod.shape[0] == 1:
        return pl.BlockSpec((1, d), lambda i: (0, 0))
    return pl.BlockSpec((tm, d), lambda i: (i, 0))


def _ada_kernel(c_ref, w_ref, b_ref, o_ref):
    c = c_ref[...]
    a = (c * jax.nn.sigmoid(c)).astype(BF16)
    o_ref[...] = jnp.dot(a, w_ref[...].astype(BF16), preferred_element_type=F32) + b_ref[...]


def _ada(c, w, b):
    bsz, d = c.shape
    n = w.shape[1]
    tn = _pick(n, TN_ADA)
    return pl.pallas_call(
        _ada_kernel,
        grid=(n // tn,),
        in_specs=[pl.BlockSpec((bsz, d), lambda j: (0, 0)),
                  pl.BlockSpec((d, tn), lambda j: (0, j)),
                  pl.BlockSpec((1, tn), lambda j: (0, j))],
        out_specs=pl.BlockSpec((bsz, tn), lambda j: (0, j)),
        out_shape=jax.ShapeDtypeStruct((bsz, n), F32),
        compiler_params=_cparams("parallel"),
        name="ada_proj",
    )(c, w, b)


def _norm_mod_kernel(x_ref, g_ref, sc_ref, sh_ref, o_ref):
    y = _rms(x_ref[...], g_ref[...])
    o_ref[...] = (y * (1.0 + sc_ref[...]) + sh_ref[...]).astype(o_ref.dtype)


def _norm_mod(x, g, sc, sh):
    m, d = x.shape
    tm = _row_tile(m, sc, sh)
    return pl.pallas_call(
        _norm_mod_kernel,
        grid=(m // tm,),
        in_specs=[pl.BlockSpec((tm, d), lambda i: (i, 0)),
                  pl.BlockSpec((1, d), lambda i: (0, 0)),
                  _mod_spec(sc, tm, d), _mod_spec(sh, tm, d)],
        out_specs=pl.BlockSpec((tm, d), lambda i: (i, 0)),
        out_shape=jax.ShapeDtypeStruct((m, d), BF16),
        compiler_params=_cparams("parallel"),
        name="norm_mod",
    )(x, g, sc, sh)


def _group_norm_kernel(a_ref, b_ref, ga_ref, gb_ref, o_ref):
    a = jnp.concatenate([a_ref[h] for h in range(a_ref.shape[0])], axis=1).astype(F32)
    da = a.shape[1]
    o_ref[:, :da] = _rms(a, ga_ref[...]).astype(o_ref.dtype)
    o_ref[:, da:] = _rms(b_ref[...].astype(F32), gb_ref[...]).astype(o_ref.dtype)


def _group_norm_concat(a, b, ga, gb):
    h, m, dh = a.shape
    da = h * dh
    db = b.shape[1]
    tm = _pick(m, TM_ROW)
    return pl.pallas_call(
        _group_norm_kernel,
        grid=(m // tm,),
        in_specs=[pl.BlockSpec((h, tm, dh), lambda i: (0, i, 0)),
                  pl.BlockSpec((tm, db), lambda i: (i, 0)),
                  pl.BlockSpec((1, da), lambda i: (0, 0)),
                  pl.BlockSpec((1, db), lambda i: (0, 0))],
        out_specs=pl.BlockSpec((tm, da + db), lambda i: (i, 0)),
        out_shape=jax.ShapeDtypeStruct((m, da + db), BF16),
        compiler_params=_cparams("parallel"),
        name="group_norm_concat",
    )(a, b, ga, gb)


def _post_mix_kernel(x_ref, y_ref, gpost_ref, gate_ref, gpre_ref, sc_ref, sh_ref, x1_ref, h_ref):
    x1 = x_ref[...] + gate_ref[...] * _rms(y_ref[...].astype(F32), gpost_ref[...])
    x1_ref[...] = x1
    h = _rms(x1, gpre_ref[...])
    h_ref[...] = (h * (1.0 + sc_ref[...]) + sh_ref[...]).astype(h_ref.dtype)


def _post_mix(x, y, gpost, gate, gpre, sc, sh):
    m, d = x.shape
    tm = _row_tile(m, gate, sc, sh)
    row = pl.BlockSpec((tm, d), lambda i: (i, 0))
    vec = pl.BlockSpec((1, d), lambda i: (0, 0))
    return pl.pallas_call(
        _post_mix_kernel,
        grid=(m // tm,),
        in_specs=[row, row, vec, _mod_spec(gate, tm, d), vec, _mod_spec(sc, tm, d), _mod_spec(sh, tm, d)],
        out_specs=[row, row],
        out_shape=[jax.ShapeDtypeStruct((m, d), F32), jax.ShapeDtypeStruct((m, d), BF16)],
        compiler_params=_cparams("parallel"),
        name="post_mix",
    )(x, y, gpost, gate, gpre, sc, sh)


def _ffn_down_kernel(a_ref, b_ref, x_ref, gpost_ref, gate_ref, o_ref, y_scr, ssq_scr, *, nj, ni):
    s = pl.program_id(0)
    row = lax.div(s, nj)
    j = lax.rem(s, nj)
    cur = lax.rem(row, 2)
    tm, tn = o_ref.shape

    @pl.when(s == 0)
    def _():
        y_scr[...] = jnp.zeros(y_scr.shape, y_scr.dtype)
        ssq_scr[...] = jnp.zeros(ssq_scr.shape, F32)

    def residual(rows):
        ms = ssq_scr[1 - cur, rows, 0:1] * (1.0 / (nj * tn))
        gate = gate_ref[rows, :] if gate_ref.shape[0] > 1 else gate_ref[...]
        y_prev = y_scr[j, rows, :].astype(F32)
        o_ref[rows, :] = x_ref[rows, :] + gate * (y_prev * lax.rsqrt(ms + EPS) * gpost_ref[...])

    k = a_ref.shape[1]
    n_k = 1
    while 2 * n_k <= min(KCHUNKS_DOWN, k // MXU_DEPTH) and tm % (2 * n_k * SUBLANES) == 0:
        n_k *= 2
    bounds = [k // MXU_DEPTH * c // n_k * MXU_DEPTH for c in range(n_k)] + [k]
    rows_per = tm // n_k

    @pl.when(row < ni)
    def _():
        y = None
        for c in range(n_k):
            ks = slice(bounds[c], bounds[c + 1])
            part = jnp.dot(a_ref[:, ks], b_ref[ks, :], preferred_element_type=F32)
            y = part if y is None else y + part
            residual(slice(c * rows_per, (c + 1) * rows_per))
        y_scr[j] = y.astype(y_scr.dtype)
        sq = jnp.broadcast_to(jnp.sum(y * y, axis=1, keepdims=True), (tm, LANES))
        ssq_scr[cur] = jnp.where(j == 0, sq, ssq_scr[cur] + sq)

    @pl.when(row == ni)
    def _():
        residual(slice(0, tm))


def _ffn_down_residual(a, b, x, gpost, gate):
    m, k = a.shape
    n = b.shape[1]
    tm = _pick(m, TM_DOWN)
    tn = _pick(n, TN_DOWN)
    ni, nj = m // tm, n // tn
    r_d = lambda s: jnp.minimum(s // nj, ni - 1)
    r_e = lambda s: jnp.maximum(s // nj - 1, 0)
    j_e = lambda s: jnp.where(s < nj, 0, s % nj)
    if gate.shape[0] == 1:
        gate_spec = pl.BlockSpec((1, tn), lambda s: (0, j_e(s)))
    else:
        gate_spec = pl.BlockSpec((tm, tn), lambda s: (r_e(s), j_e(s)))
    return pl.pallas_call(
        functools.partial(_ffn_down_kernel, nj=nj, ni=ni),
        grid=((ni + 1) * nj,),
        in_specs=[pl.BlockSpec((tm, k), lambda s: (r_d(s), 0)),
                  pl.BlockSpec((k, tn), lambda s: (0, jnp.where(s // nj < ni, s % nj, nj - 1))),
                  pl.BlockSpec((tm, tn), lambda s: (r_e(s), j_e(s))),
                  pl.BlockSpec((1, tn), lambda s: (0, j_e(s))),
                  gate_spec],
        out_specs=pl.BlockSpec((tm, tn), lambda s: (r_e(s), j_e(s))),
        out_shape=jax.ShapeDtypeStruct((m, n), F32),
        scratch_shapes=[pltpu.VMEM((nj, tm, tn), BF16), pltpu.VMEM((2, tm, LANES), F32)],
        compiler_params=_cparams("arbitrary"),
        name="ffn_down",
    )(a, b, x, gpost, gate)


def _matmul_kernel(a_ref, b_ref, *o_refs):
    acc = jnp.dot(a_ref[...], b_ref[...], preferred_element_type=F32)
    for o_ref in o_refs:
        if len(o_ref.shape) == 3:
            for h in range(o_ref.shape[0]):
                o_ref[h] = acc[:, h * LANES:(h + 1) * LANES].astype(o_ref.dtype)
        else:
            o_ref[...] = acc.astype(o_ref.dtype)


def _matmul(a, b, col0, n, outs, tm_pref=None, tn_pref=None, name="matmul"):
    m, k = a.shape
    tm = _pick(m, TM_MM if tm_pref is None else tm_pref)
    tn = _pick(n, TN_MM if tn_pref is None else tn_pref)
    assert col0 % tn == 0 and tn % LANES == 0
    off = col0 // tn
    hpt = tn // LANES
    out_specs, out_shape = [], []
    for dt, head_major in outs:
        if head_major:
            out_specs.append(pl.BlockSpec((hpt, tm, LANES), lambda i, j: (j, i, 0)))
            out_shape.append(jax.ShapeDtypeStruct((n // LANES, m, LANES), dt))
        else:
            out_specs.append(pl.BlockSpec((tm, tn), lambda i, j: (i, j)))
            out_shape.append(jax.ShapeDtypeStruct((m, n), dt))
    return pl.pallas_call(
        _matmul_kernel,
        grid=(m // tm, n // tn),
        in_specs=[pl.BlockSpec((tm, k), lambda i, j: (i, 0)),
                  pl.BlockSpec((k, tn), lambda i, j: (0, j + off))],
        out_specs=out_specs,
        out_shape=out_shape,
        compiler_params=_cparams("parallel", "arbitrary"),
        name=name,
    )(a, b)


def _cumsum_ones(cw):
    j = lax.broadcasted_iota(jnp.int32, (2 * cw, 2 * cw), 0)
    s = lax.broadcasted_iota(jnp.int32, (2 * cw, 2 * cw), 1)
    jj = jnp.where(j >= cw, j - cw, j)
    return jnp.where((s >= cw) | (jj > s), 1.0, 0.0).astype(BF16)


def _sb_tile(q, k, v, c, uu, mask_offset):
    return _sb_tiles([(q, k, v, c)], uu, mask_offset)[0]


def _sb_tiles(items, uu, mask_offset):
    scored = [_sb_scores(q, k, uu.shape[0] // 2, mask_offset) for q, k, _, _ in items]
    xs = [x for _, x in scored]
    r = jnp.dot(xs[0] if len(xs) == 1 else jnp.concatenate(xs, axis=0), uu, preferred_element_type=F32)
    outs, row = [], 0
    for (_, _, v, c), (chunks, x) in zip(items, scored):
        outs.append(_sb_weights(chunks, r[row:row + x.shape[0]], c, v))
        row += x.shape[0]
    return outs


def _sb_scores(q, k, cw, mask_offset):
    bq, dh = q.shape
    nc = k.shape[0] // cw
    z = lax.dot_general(q, k, (((1,), (1,)), ((), ())), preferred_element_type=F32) * (dh ** -0.5)
    chunks, xs = [], []
    for ci in range(nc):
        masked = mask_offset is not None and (ci + 1) * cw > mask_offset
        row0 = 0
        if masked:
            row0 = min(max(ci * cw - mask_offset + 1, 0), bq - BF16_ROWS) // BF16_ROWS * BF16_ROWS
        zc = z[row0:, ci * cw:(ci + 1) * cw]
        log_beta = jnp.minimum(zc, 0.0) - jnp.log(1.0 + jnp.exp(-jnp.abs(zc)))
        log_keep = log_beta - zc
        vis = None
        if masked:
            t_idx = lax.broadcasted_iota(jnp.int32, zc.shape, 0)
            s_idx = lax.broadcasted_iota(jnp.int32, zc.shape, 1)
            vis = s_idx < t_idx + (row0 + mask_offset - ci * cw)
            log_keep = jnp.where(vis, log_keep, 0.0)
        hi = log_keep.astype(BF16)
        lo = (log_keep - hi.astype(F32)).astype(BF16)
        chunks.append((log_beta, vis, row0))
        xs.append(jnp.concatenate([hi, lo], axis=1))
    return chunks, xs[0] if nc == 1 else jnp.concatenate(xs, axis=0)


def _sb_weights(chunks, r, c, v):
    cw = chunks[0][0].shape[1]
    ends, row = [], 0
    for log_beta, _, _ in chunks:
        row += log_beta.shape[0]
        ends.append(row)
    ws = [None] * len(chunks)
    for ci in reversed(range(len(chunks))):
        log_beta, vis, row0 = chunks[ci]
        rc = r[ends[ci] - log_beta.shape[0]:ends[ci]]
        w = jnp.exp(log_beta + rc[:, :cw] + c[row0:])
        if vis is not None:
            w = jnp.where(vis, w, 0.0)
        w = w.astype(BF16)
        c_new = c[row0:] + rc[:, cw:]
        if row0:
            w = jnp.concatenate([jnp.zeros((row0, cw), BF16), w], axis=0)
            c_new = jnp.concatenate([c[:row0], c_new], axis=0)
        ws[ci] = w
        c = c_new
    w = ws[0] if len(ws) == 1 else jnp.concatenate(ws, axis=1)
    return jnp.dot(w, v, preferred_element_type=F32), c


def _sb_prompt_kernel(q_ref, k_ref, v_ref, o_ref, acc_scr, c_scr, *, bq, n_par):
    n_blocks = q_ref.shape[0] // bq
    n_groups = (n_blocks - 1) // n_par
    uu = _cumsum_ones(LANES)
    c0 = jnp.zeros((bq, LANES), F32)

    acc, _ = _sb_tile(q_ref[0:bq, :], k_ref[0:bq, :], v_ref[0:bq, :], c0, uu, 0)
    o_ref[0:bq, :] = acc.astype(o_ref.dtype)

    def group(g, carry):
        items, q0s = [], []
        for u in range(n_par):
            q0 = pl.multiple_of((1 + g + u * n_groups) * bq, bq)
            k0 = pl.multiple_of(q0 - bq, bq)
            items.append((q_ref[pl.ds(q0, bq), :], k_ref[pl.ds(k0, 2 * bq), :], v_ref[pl.ds(k0, 2 * bq), :], c0))
            q0s.append(q0)
        res = _sb_tiles(items, uu, bq)
        for q0, (acc, _) in zip(q0s, res):
            o_ref[pl.ds(q0, bq), :] = acc.astype(o_ref.dtype)
        c_max = res[0][1]
        for _, c in res[1:]:
            c_max = jnp.maximum(c_max, c)

        @pl.when(jnp.max(c_max) > LOG_WEIGHT_ZERO)
        def _():
            for u, (acc, c) in enumerate(res):
                acc_scr[u] = acc
                c_scr[u] = c

            def older_keys(u, carry):
                qb = 1 + g + u * n_groups
                q0 = pl.multiple_of(qb * bq, bq)
                k0 = pl.multiple_of(q0 - bq, bq)
                q = q_ref[pl.ds(q0, bq), :]
                n_wide = (qb - 1) // 2

                def more(st):
                    j, _, c = st
                    return jnp.logical_and(j < n_wide, jnp.max(c) > LOG_WEIGHT_ZERO)

                def wide_tile(st):
                    j, acc, c = st
                    ks = pl.multiple_of(k0 - 2 * bq * (j + 1), bq)
                    da, c = _sb_tile(q, k_ref[pl.ds(ks, 2 * bq), :], v_ref[pl.ds(ks, 2 * bq), :], c, uu, None)
                    return j + 1, acc + da, c

                j, acc, c = lax.while_loop(more, wide_tile, (jnp.int32(0), acc_scr[u], c_scr[u]))
                need_last = jnp.logical_and(jnp.logical_and((qb - 1) % 2 == 1, j == n_wide),
                                            jnp.max(c) > LOG_WEIGHT_ZERO)

                def last_tile(acc, c):
                    da, _ = _sb_tile(q, k_ref[0:bq, :], v_ref[0:bq, :], c, uu, None)
                    return acc + da

                acc = lax.cond(need_last, last_tile, lambda acc, c: acc, acc, c)
                o_ref[pl.ds(q0, bq), :] = acc.astype(o_ref.dtype)
                return carry

            lax.fori_loop(0, n_par, older_keys, 0)

        return carry

    lax.fori_loop(0, n_groups, group, 0)


def _sb_prompt(q, k, v):
    h, t, dh = q.shape
    bq = _pick(t, BQ_ATT)
    n_rest = t // bq - 1
    n_par = next(u for u in (4, 3, 2, 1) if n_rest % u == 0)
    spec = pl.BlockSpec((None, t, dh), lambda i: (i, 0, 0))
    return pl.pallas_call(
        functools.partial(_sb_prompt_kernel, bq=bq, n_par=n_par),
        grid=(h,),
        in_specs=[spec, spec, spec],
        out_specs=spec,
        out_shape=jax.ShapeDtypeStruct((h, t, dh), BF16),
        scratch_shapes=[pltpu.VMEM((n_par, bq, dh), F32), pltpu.VMEM((n_par, bq, LANES), F32)],
        compiler_params=_cparams("parallel"),
        name="sb_prompt",
    )(q, k, v)


def _sb_sample_kernel(q_ref, kn_ref, vn_ref, kp_ref, vp_ref, o_ref, acc_scr, c_scr, *, tk):
    n_heads, tq, dh = q_ref.shape
    kt = pl.program_id(1)
    uu = _cumsum_ones(LANES)

    @pl.when(kt == 0)
    def _():
        pad = jnp.zeros((LANES - tq, dh), BF16)
        c0 = jnp.zeros((tq, LANES), F32)
        items = [(q_ref[h], jnp.concatenate([kn_ref[h], pad], axis=0),
                  jnp.concatenate([vn_ref[h], pad], axis=0), c0) for h in range(n_heads)]
        for h, (acc, c) in enumerate(_sb_tiles(items, uu, 0)):
            acc_scr[h] = acc
            c_scr[h] = c

    @pl.when(jnp.max(c_scr[...]) > LOG_WEIGHT_ZERO)
    def _():
        items = [(q_ref[h], kp_ref[0, pl.ds(h, tk, stride=n_heads), :].astype(BF16),
                  vp_ref[0, pl.ds(h, tk, stride=n_heads), :].astype(BF16), c_scr[h]) for h in range(n_heads)]
        for h, (da, c) in enumerate(_sb_tiles(items, uu, None)):
            acc_scr[h] = acc_scr[h] + da
            c_scr[h] = c

    @pl.when(kt == pl.num_programs(1) - 1)
    def _():
        for h in range(n_heads):
            o_ref[h] = acc_scr[h].astype(o_ref.dtype)


def _sb_sample(q, kn, vn, kp, vp, tq):
    h, m, dh = q.shape
    bsz = m // tq
    p = kp.shape[1] // h
    tk = _pick(p, TK_SAMPLE)
    n_kt = p // tk
    assert tq <= LANES and tk % LANES == 0
    new = pl.BlockSpec((h, tq, dh), lambda b, kt: (0, b, 0))
    past = pl.BlockSpec((1, tk * h, dh), lambda b, kt: (b, n_kt - 1 - kt, 0))
    return pl.pallas_call(
        functools.partial(_sb_sample_kernel, tk=tk),
        grid=(bsz, n_kt),
        in_specs=[new, new, new, past, past],
        out_specs=new,
        out_shape=jax.ShapeDtypeStruct((h, m, dh), BF16),
        scratch_shapes=[pltpu.VMEM((h, tq, dh), F32), pltpu.VMEM((h, tq, LANES), F32)],
        compiler_params=_cparams("parallel", "arbitrary"),
        name="sb_sample",
    )(q, kn, vn, kp, vp)


def _shift_rows(x, prev8, s):
    n, c = x.shape
    groups = n // SUBLANES
    rot = pltpu.roll(x.reshape(groups, SUBLANES, c), s, 1)
    before = pltpu.roll(prev8, s, 0).reshape(1, SUBLANES, c)
    if groups > 1:
        before = jnp.concatenate([before, rot[:groups - 1]], axis=0)
    sub = lax.broadcasted_iota(jnp.int32, rot.shape, 1)
    return jnp.where(sub < s, before, rot).reshape(n, c)


def _causal_conv(x, prev8, w, b):
    width = w.shape[0]
    y = b + w[width - 1:width] * x
    for j in range(width - 1):
        y = y + w[j:j + 1] * _shift_rows(x, prev8, width - 1 - j)
    return y


def _linear_scan(a, u, h_in):
    n, c = a.shape
    a = a.reshape(n // SUBLANES, SUBLANES, c)
    u = u.reshape(n // SUBLANES, SUBLANES, c)
    sub = lax.broadcasted_iota(jnp.int32, a.shape, 1)
    d = 1
    while d < SUBLANES:
        keep = sub >= d
        a_sh = jnp.where(keep, pltpu.roll(a, d, 1), 1.0)
        u_sh = jnp.where(keep, pltpu.roll(u, d, 1), 0.0)
        u = u + a * u_sh
        a = a * a_sh
        d *= 2
    a = a.reshape(n, c)
    u = u.reshape(n, c)
    hs = []
    for g in range(n // SUBLANES):
        rows = slice(g * SUBLANES, (g + 1) * SUBLANES)
        hg = a[rows] * h_in + u[rows]
        hs.append(hg)
        h_in = hg[SUBLANES - 1:SUBLANES, :]
    return hs[0] if len(hs) == 1 else jnp.concatenate(hs, axis=0)


def _lru_kernel(xl_ref, gl_ref, prev_ref, h0_ref, cw_ref, cb_ref, wa_ref, ba_ref, wx_ref, bx_ref, lam_ref,
                o_ref, conv8_ref, h8_ref, prev_scr, h_scr, *, n_seq, piece):
    t = pl.program_id(1)

    @pl.when(t == 0)
    def _():
        prev_scr[...] = prev_ref[...]
        h_scr[...] = h0_ref[...]

    lam = lam_ref[...]
    softplus_neg_lam = jnp.maximum(-lam, 0.0) + jnp.log1p(jnp.exp(-jnp.abs(lam)))
    wa = wa_ref[0]
    wx = wx_ref[0]
    seq_rows = xl_ref.shape[0] // n_seq
    for s in range(n_seq):
        prev8 = prev_scr[s]
        h8 = h_scr[s]
        for pc in range(seq_rows // piece):
            rows = slice(s * seq_rows + pc * piece, s * seq_rows + (pc + 1) * piece)
            x = xl_ref[rows, :]
            xc = _causal_conv(x, prev8, cw_ref[...], cb_ref[...])
            xcb = xc.astype(BF16)
            r = jax.nn.sigmoid(jnp.dot(xcb, wa, preferred_element_type=F32) + ba_ref[...])
            i = jax.nn.sigmoid(jnp.dot(xcb, wx, preferred_element_type=F32) + bx_ref[...])
            log_a = -LRU_C * r * softplus_neg_lam
            a = jnp.exp(log_a)
            u = jnp.sqrt(-jnp.tanh(log_a) * (a * a + 1.0)) * (i * xc)
            hs = _linear_scan(a, u, h8[SUBLANES - 1:SUBLANES, :])
            o_ref[rows, :] = (hs * _gelu_tanh(gl_ref[rows, :])).astype(o_ref.dtype)
            prev8 = x[piece - SUBLANES:, :]
            h8 = hs[piece - SUBLANES:, :]
        prev_scr[s] = prev8
        h_scr[s] = h8
        conv8_ref[s] = prev8
        h8_ref[s] = h8


def _lru(xg, prev8, h08, conv_w, conv_b, w_a, b_a, w_x, b_x, lam, n_seq, seq_len):
    m, c2 = xg.shape
    c = c2 // 2
    nb = c // LANES
    assert w_a.shape == (nb, LANES, LANES) and seq_len % SUBLANES == 0 and seq_len >= conv_w.shape[0] - 1
    if n_seq == 1:
        tt = _pick(seq_len, TT_LRU)
        s_t, piece = 1, _pick(tt, PC_LRU)
    else:
        tt = m
        s_t, piece = n_seq, seq_len
    nt = m // tt
    width = conv_w.shape[0]
    vec = pl.BlockSpec((1, LANES), lambda n, t: (0, n))
    st8 = pl.BlockSpec((s_t, SUBLANES, LANES), lambda n, t: (0, 0, n))
    out8 = pl.BlockSpec((s_t, SUBLANES, LANES), lambda n, t: (t, 0, n))
    gate_w = pl.BlockSpec((1, LANES, LANES), lambda n, t: (n, 0, 0))
    return pl.pallas_call(
        functools.partial(_lru_kernel, n_seq=s_t, piece=piece),
        grid=(nb, nt),
        in_specs=[pl.BlockSpec((tt, LANES), lambda n, t: (t, n)),
                  pl.BlockSpec((tt, LANES), lambda n, t: (t, n + nb)),
                  st8, st8,
                  pl.BlockSpec((width, LANES), lambda n, t: (0, n)), vec,
                  gate_w, vec, gate_w, vec, vec],
        out_specs=[pl.BlockSpec((tt, LANES), lambda n, t: (t, n)), out8, out8],
        out_shape=[jax.ShapeDtypeStruct((m, c), BF16),
                   jax.ShapeDtypeStruct((nt * s_t, SUBLANES, c), F32),
                   jax.ShapeDtypeStruct((nt * s_t, SUBLANES, c), F32)],
        scratch_shapes=[pltpu.VMEM((s_t, SUBLANES, LANES), F32), pltpu.VMEM((s_t, SUBLANES, LANES), F32)],
        compiler_params=_cparams("parallel", "arbitrary"),
        name="rg_lru",
    )(xg, xg, prev8, h08, conv_w, conv_b, w_a, b_a, w_x, b_x, lam)


def _ffn_up_kernel(h_ref, wg_ref, wv_ref, pg_ref, pv_ref, cwg_ref, cwv_ref, cbg_ref, cbv_ref,
                   act_ref, lastg_ref, lastv_ref, raw_g, raw_v, carry_g, carry_v, *, n_seq, nj):
    s = pl.program_id(0)
    tile_e = jnp.maximum(s - 1, 0)
    i_e = lax.div(tile_e, nj)
    j_e = lax.rem(tile_e, nj)

    @pl.when(s == 0)
    def _():
        raw_g[1] = jnp.zeros(raw_g.shape[1:], F32)
        raw_v[1] = jnp.zeros(raw_v.shape[1:], F32)

    @pl.when(i_e == 0)
    def _():
        carry_g[j_e] = pg_ref[...]
        carry_v[j_e] = pv_ref[...]

    valid = s > 0
    tm = h_ref.shape[0]
    seq_rows = tm // n_seq
    piece = min(seq_rows, PIECE_FFN)

    def step(cur, prv):
        pieces = [(sq, p0) for sq in range(n_seq) for p0 in range(sq * seq_rows, (sq + 1) * seq_rows, piece)]
        n_k = max(1, min(KCHUNKS_FFN, len(pieces), h_ref.shape[1] // MXU_DEPTH))
        kc_size = h_ref.shape[1] // n_k
        prev_g = [carry_g[j_e, sq] for sq in range(n_seq)]
        prev_v = [carry_v[j_e, sq] for sq in range(n_seq)]
        old_g, old_v = list(prev_g), list(prev_v)
        acc_g = acc_v = None
        done = 0
        for kc in range(n_k):
            ks = slice(kc * kc_size, (kc + 1) * kc_size)
            h = h_ref[:, ks]
            part_g = jnp.dot(h, wg_ref[ks, :].astype(BF16), preferred_element_type=F32)
            part_v = jnp.dot(h, wv_ref[ks, :].astype(BF16), preferred_element_type=F32)
            acc_g = part_g if acc_g is None else acc_g + part_g
            acc_v = part_v if acc_v is None else acc_v + part_v
            upto = len(pieces) * (kc + 1) // n_k
            for sq, p0 in pieces[done:upto]:
                xg = raw_g[prv, p0:p0 + piece, :]
                xv = raw_v[prv, p0:p0 + piece, :]
                yg = _causal_conv(xg, prev_g[sq], cwg_ref[...], cbg_ref[...])
                yv = _causal_conv(xv, prev_v[sq], cwv_ref[...], cbv_ref[...])
                act_ref[p0:p0 + piece, :] = (_gelu_tanh(yg) * yv).astype(act_ref.dtype)
                prev_g[sq] = xg[piece - SUBLANES:, :]
                prev_v[sq] = xv[piece - SUBLANES:, :]
            done = upto
        raw_g[cur] = acc_g
        raw_v[cur] = acc_v
        for sq in range(n_seq):
            carry_g[j_e, sq] = jnp.where(valid, prev_g[sq], old_g[sq])
            carry_v[j_e, sq] = jnp.where(valid, prev_v[sq], old_v[sq])
            lastg_ref[sq] = prev_g[sq]
            lastv_ref[sq] = prev_v[sq]

    slot = lax.rem(s, 2)
    pl.when(slot == 0)(lambda: step(0, 1))
    pl.when(slot == 1)(lambda: step(1, 0))


def _ffn_up(h, w_up, prev8, conv_w, conv_b, n_seq, seq_len):
    m, d = h.shape
    f = w_up.shape[1] // 2
    tn = _pick(f, TN_FFN)
    assert seq_len % SUBLANES == 0 and seq_len >= conv_w.shape[0] - 1
    if n_seq == 1:
        tm = _pick(seq_len, TM_FFN)
        s_t = 1
    else:
        tm = m
        s_t = n_seq
    ni, nj = m // tm, f // tn
    n_tiles = ni * nj
    width = conv_w.shape[0]
    i_d = lambda s: jnp.minimum(s, n_tiles - 1) // nj
    j_d = lambda s: jnp.minimum(s, n_tiles - 1) % nj
    i_e = lambda s: jnp.maximum(s - 1, 0) // nj
    j_e = lambda s: jnp.maximum(s - 1, 0) % nj
    st8 = lambda off: pl.BlockSpec((s_t, SUBLANES, tn), lambda s: (0, 0, j_e(s) + off))
    out8 = pl.BlockSpec((s_t, SUBLANES, tn), lambda s: (i_e(s), 0, j_e(s)))
    wspec = lambda off: pl.BlockSpec((d, tn), lambda s: (0, j_d(s) + off))
    cw = lambda off: pl.BlockSpec((width, tn), lambda s: (0, j_e(s) + off))
    cb = lambda off: pl.BlockSpec((1, tn), lambda s: (0, j_e(s) + off))
    return pl.pallas_call(
        functools.partial(_ffn_up_kernel, n_seq=s_t, nj=nj),
        grid=(n_tiles + 1,),
        in_specs=[pl.BlockSpec((tm, d), lambda s: (i_d(s), 0)),
                  wspec(0), wspec(nj), st8(0), st8(nj), cw(0), cw(nj), cb(0), cb(nj)],
        out_specs=[pl.BlockSpec((tm, tn), lambda s: (i_e(s), j_e(s))), out8, out8],
        out_shape=[jax.ShapeDtypeStruct((m, f), BF16),
                   jax.ShapeDtypeStruct((ni * s_t, SUBLANES, f), F32),
                   jax.ShapeDtypeStruct((ni * s_t, SUBLANES, f), F32)],
        scratch_shapes=[pltpu.VMEM((2, tm, tn), F32), pltpu.VMEM((2, tm, tn), F32),
                        pltpu.VMEM((nj, s_t, SUBLANES, tn), F32), pltpu.VMEM((nj, s_t, SUBLANES, tn), F32)],
        compiler_params=_cparams("arbitrary"),
        name="ffn_up",
    )(h, w_up, w_up, prev8, prev8, conv_w, conv_w, conv_b, conv_b)


def _state8(state, n_rows):
    return jnp.pad(state.astype(F32), ((0, 0), (SUBLANES - n_rows, 0), (0, 0)))


def _layer(x, mods, p, past, n_seq, seq_len):
    m, d = x.shape
    d_sb = d // 2
    d_lru = d // 2
    sh_m, sc_m, g_m, sh_f, sc_f, g_f = mods
    k_past, v_past, conv_lru_prev, h_prev, conv_ffn_prev = past

    h = _norm_mod(x, p['g_pre_mix'], sc_m, sh_m)
    (q,) = _matmul(h, p['w_in'], 0, d_sb, [(BF16, True)], name="proj_q")
    k32, k16 = _matmul(h, p['w_in'], d_sb, d_sb, [(F32, False), (BF16, True)], name="proj_k")
    v32, v16 = _matmul(h, p['w_in'], 2 * d_sb, d_sb, [(F32, False), (BF16, True)], name="proj_v")
    (xg,) = _matmul(h, p['w_in'], 3 * d_sb, 2 * d_lru, [(F32, False)], name="proj_lru")

    if k_past is None:
        o_sb = _sb_prompt(q, k16, v16)
    else:
        o_sb = _sb_sample(q, k16, v16, k_past, v_past, seq_len)

    w_c = p['w_conv_lru']
    o_lru, conv8, h8 = _lru(xg, _state8(conv_lru_prev, w_c.shape[0] - 1),
                            jnp.broadcast_to(h_prev.astype(F32)[:, None, :], (n_seq, SUBLANES, d_lru)),
                            w_c, p['b_conv_lru'], p['w_gate_a'], p['b_gate_a'], p['w_gate_x'], p['b_gate_x'],
                            p['lru_lambda'], n_seq, seq_len)
    conv_lru_new = conv8[-n_seq:, SUBLANES - (w_c.shape[0] - 1):, :]
    h_last = h8[-n_seq:, SUBLANES - 1, :]

    mix_in = _group_norm_concat(o_sb, o_lru, p['g_grp_attn'], p['g_grp_lru'])
    (mixed,) = _matmul(mix_in, p['w_out'], 0, d, [(BF16, False)], name="proj_out")
    x1, h2 = _post_mix(x, mixed, p['g_post_mix'], g_m, p['g_pre_ffn'], sc_f, sh_f)

    w_f = p['w_conv_ffn']
    act, last_g, last_v = _ffn_up(h2, p['w_up'], _state8(conv_ffn_prev, w_f.shape[0] - 1),
                                  w_f, p['b_conv_ffn'], n_seq, seq_len)
    keep = SUBLANES - (w_f.shape[0] - 1)
    conv_ffn_new = jnp.concatenate([last_g[-n_seq:, keep:, :], last_v[-n_seq:, keep:, :]], axis=-1)
    out = _ffn_down_residual(act, p['w_down'], x1, p['g_post_ffn'], g_f)
    return out, (k32, v32, conv_lru_new, h_last, conv_ffn_new)


def kernel(x_prompt, x_sample, c_prompt, c_sample, cache_k, cache_v, state_conv_lru, state_lru, state_conv_ffn, w_ada, b_ada, g_pre_mix, w_in, w_conv_lru, b_conv_lru, w_gate_a, b_gate_a, w_gate_x, b_gate_x, lru_lambda, g_grp_attn, g_grp_lru, w_out, g_post_mix, g_pre_ffn, w_up, w_conv_ffn, b_conv_ffn, w_down, g_post_ffn):
    depth = w_ada.shape[0]
    bp, tp, d = x_prompt.shape
    bs, ts, _ = x_sample.shape
    assert bp == 1, "the prompt kernels treat the prompt as one sequence"
    d_sb = d // 2
    n_heads = d_sb // HEAD_DIM
    n_mod = w_ada.shape[2] // d

    xp = x_prompt.reshape(bp * tp, d)
    xs = x_sample.reshape(bs * ts, d)
    n_c = bp + bs
    n_c_pad = -(-n_c // SUBLANES) * SUBLANES
    c_all = jnp.concatenate([c_prompt, c_sample, jnp.zeros((n_c_pad - n_c, d), F32)], axis=0)

    st_p = [[], [], [], [], []]
    st_s = [[], [], [], [], []]
    for l in range(depth):
        row = lambda a: a[l][None, :]
        p = {
            'g_pre_mix': row(g_pre_mix), 'w_in': w_in[l].astype(BF16),
            'w_conv_lru': w_conv_lru[l], 'b_conv_lru': row(b_conv_lru),
            'w_gate_a': w_gate_a[l].astype(BF16), 'b_gate_a': row(b_gate_a),
            'w_gate_x': w_gate_x[l].astype(BF16), 'b_gate_x': row(b_gate_x),
            'lru_lambda': row(lru_lambda), 'g_grp_attn': row(g_grp_attn), 'g_grp_lru': row(g_grp_lru),
            'w_out': w_out[l].astype(BF16), 'g_post_mix': row(g_post_mix), 'g_pre_ffn': row(g_pre_ffn),
            'w_up': w_up[l], 'w_conv_ffn': w_conv_ffn[l], 'b_conv_ffn': row(b_conv_ffn),
            'w_down': w_down[l].astype(BF16), 'g_post_ffn': row(g_post_ffn),
        }
        mod = _ada(c_all, w_ada[l], row(b_ada))
        mods_p = [mod[0:1, i * d:(i + 1) * d] for i in range(n_mod)]
        mods_s = [jnp.repeat(mod[bp:bp + bs, i * d:(i + 1) * d], ts, axis=0) for i in range(n_mod)]

        d_lru = d // 2
        f2 = w_up.shape[2]
        prompt_past = (None, None, jnp.zeros((bp, w_conv_lru.shape[1] - 1, d_lru), F32),
                       jnp.zeros((bp, d_lru), F32), jnp.zeros((bp, w_conv_ffn.shape[1] - 1, f2), F32))
        xp, new_p = _layer(xp, mods_p, p, prompt_past, bp, tp)
        past_len = cache_k.shape[2]
        sample_past = (cache_k[l].reshape(bs, past_len * n_heads, HEAD_DIM),
                       cache_v[l].reshape(bs, past_len * n_heads, HEAD_DIM),
                       state_conv_lru[l], state_lru[l], state_conv_ffn[l])
        xs, new_s = _layer(xs, mods_s, p, sample_past, bs, ts)
        for j in range(5):
            st_p[j].append(new_p[j])
            st_s[j].append(new_s[j])

    def pack(st, b, t):
        k = jnp.stack(st[0], axis=0).reshape(depth, b, t, n_heads, HEAD_DIM)
        v = jnp.stack(st[1], axis=0).reshape(depth, b, t, n_heads, HEAD_DIM)
        return k, v, jnp.stack(st[2], axis=0), jnp.stack(st[3], axis=0), jnp.stack(st[4], axis=0)

    k_p, v_p, cl_p, h_p, cf_p = pack(st_p, bp, tp)
    k_s, v_s, cl_s, h_s, cf_s = pack(st_s, bs, ts)
    return (xp.reshape(bp, tp, d), xs.reshape(bs, ts, d), k_p, v_p, cl_p, h_p, cf_p, k_s, v_s, cl_s, h_s, cf_s)
```

```python
import functools
import math

import jax
import jax.numpy as jnp
from jax import lax
from jax.experimental import pallas as pl
from jax.experimental.pallas import tpu as pltpu

F32 = jnp.float32
BF16 = jnp.bfloat16

EPS = 1e-6
LRU_C = 8.0
HEAD_DIM = 128
LANES = 128
SUBLANES = 8
MXU_DEPTH = 256
BF16_ROWS = 16
VMEM_LIMIT = 56 * 1024 * 1024
LOG_WEIGHT_ZERO = -110.0

TM_ROW = 256
TM_MM = 1024
TN_MM = 1024
TN_ADA = 512
TM_FFN = 1024
TN_FFN = 256
PIECE_FFN = 64
KCHUNKS_FFN = 16
TM_DOWN = 512
TN_DOWN = 512
KCHUNKS_DOWN = 4
TT_LRU = 1024
PC_LRU = 256
BQ_ATT = 256
TK_SAMPLE = 512


def _pick(n, pref):
    if n <= pref:
        return n
    t = pref
    while n % t:
        t //= 2
    assert t >= SUBLANES, (n, pref)
    return t


def _cparams(*sem, fuse_inputs=None):
    return pltpu.CompilerParams(dimension_semantics=sem, vmem_limit_bytes=VMEM_LIMIT,
                                allow_input_fusion=fuse_inputs)


def _gelu_tanh(x):
    return 0.5 * x * (1.0 + jnp.tanh(math.sqrt(2.0 / math.pi) * (x + 0.044715 * (x * x * x))))


def _rms(x, g):
    ms = jnp.mean(x * x, axis=-1, keepdims=True)
    return x * lax.rsqrt(ms + EPS) * g


def _row_tile(m, *mods):
    per_row = any(mod.shape[0] != 1 for mod in mods)
    return _pick(m, TM_ROW // 2 if per_row else TM_ROW)


def _mod_spec(mod, tm, d):
    if mod.shape[0] == 1:
        return pl.BlockSpec((1, d), lambda i: (0, 0))
    return pl.BlockSpec((tm, d), lambda i: (i, 0))


def _ada_kernel(c_ref, w_ref, b_ref, o_ref):
    c = c_ref[...]
    a = (c * jax.nn.sigmoid(c)).astype(BF16)
    o_ref[...] = jnp.dot(a, w_ref[...].astype(BF16), preferred_element_type=F32) + b_ref[...]


def _ada(c, w, b):
    bsz, d = c.shape
    n = w.shape[1]
    tn = _pick(n, TN_ADA)
    return pl.pallas_call(
        _ada_kernel,
        grid=(n // tn,),
        in_specs=[pl.BlockSpec((bsz, d), lambda j: (0, 0)),
                  pl.BlockSpec((d, tn), lambda j: (0, j)),
                  pl.BlockSpec((1, tn), lambda j: (0, j))],
        out_specs=pl.BlockSpec((bsz, tn), lambda j: (0, j)),
        out_shape=jax.ShapeDtypeStruct((bsz, n), F32),
        compiler_params=_cparams("parallel"),
        name="ada_proj",
    )(c, w, b)


def _norm_mod_kernel(x_ref, g_ref, sc_ref, sh_ref, o_ref):
    y = _rms(x_ref[...], g_ref[...])
    o_ref[...] = (y * (1.0 + sc_ref[...]) + sh_ref[...]).astype(o_ref.dtype)


def _norm_mod(x, g, sc, sh):
    m, d = x.shape
    tm = _row_tile(m, sc, sh)
    return pl.pallas_call(
        _norm_mod_kernel,
        grid=(m // tm,),
        in_specs=[pl.BlockSpec((tm, d), lambda i: (i, 0)),
                  pl.BlockSpec((1, d), lambda i: (0, 0)),
                  _mod_spec(sc, tm, d), _mod_spec(sh, tm, d)],
        out_specs=pl.BlockSpec((tm, d), lambda i: (i, 0)),
        out_shape=jax.ShapeDtypeStruct((m, d), BF16),
        compiler_params=_cparams("parallel"),
        name="norm_mod",
    )(x, g, sc, sh)


def _group_norm_kernel(a_ref, b_ref, ga_ref, gb_ref, o_ref):
    a = jnp.concatenate([a_ref[h] for h in range(a_ref.shape[0])], axis=1).astype(F32)
    da = a.shape[1]
    o_ref[:, :da] = _rms(a, ga_ref[...]).astype(o_ref.dtype)
    o_ref[:, da:] = _rms(b_ref[...].astype(F32), gb_ref[...]).astype(o_ref.dtype)


def _group_norm_concat(a, b, ga, gb):
    h, m, dh = a.shape
    da = h * dh
    db = b.shape[1]
    tm = _pick(m, TM_ROW)
    return pl.pallas_call(
        _group_norm_kernel,
        grid=(m // tm,),
        in_specs=[pl.BlockSpec((h, tm, dh), lambda i: (0, i, 0)),
                  pl.BlockSpec((tm, db), lambda i: (i, 0)),
                  pl.BlockSpec((1, da), lambda i: (0, 0)),
                  pl.BlockSpec((1, db), lambda i: (0, 0))],
        out_specs=pl.BlockSpec((tm, da + db), lambda i: (i, 0)),
        out_shape=jax.ShapeDtypeStruct((m, da + db), BF16),
        compiler_params=_cparams("parallel"),
        name="group_norm_concat",
    )(a, b, ga, gb)


def _post_mix_kernel(x_ref, y_ref, gpost_ref, gate_ref, gpre_ref, sc_ref, sh_ref, x1_ref, h_ref):
    x1 = x_ref[...] + gate_ref[...] * _rms(y_ref[...].astype(F32), gpost_ref[...])
    x1_ref[...] = x1
    h = _rms(x1, gpre_ref[...])
    h_ref[...] = (h * (1.0 + sc_ref[...]) + sh_ref[...]).astype(h_ref.dtype)


def _post_mix(x, y, gpost, gate, gpre, sc, sh):
    m, d = x.shape
    tm = _row_tile(m, gate, sc, sh)
    row = pl.BlockSpec((tm, d), lambda i: (i, 0))
    vec = pl.BlockSpec((1, d), lambda i: (0, 0))
    return pl.pallas_call(
        _post_mix_kernel,
        grid=(m // tm,),
        in_specs=[row, row, vec, _mod_spec(gate, tm, d), vec, _mod_spec(sc, tm, d), _mod_spec(sh, tm, d)],
        out_specs=[row, row],
        out_shape=[jax.ShapeDtypeStruct((m, d), F32), jax.ShapeDtypeStruct((m, d), BF16)],
        compiler_params=_cparams("parallel"),
        name="post_mix",
    )(x, y, gpost, gate, gpre, sc, sh)


def _ffn_down_kernel(a_ref, b_ref, x_ref, gpost_ref, gate_ref, o_ref, y_scr, ssq_scr, *, nj, ni):
    s = pl.program_id(0)
    row = lax.div(s, nj)
    j = lax.rem(s, nj)
    cur = lax.rem(row, 2)
    tm, tn = o_ref.shape

    @pl.when(s == 0)
    def _():
        y_scr[...] = jnp.zeros(y_scr.shape, y_scr.dtype)
        ssq_scr[...] = jnp.zeros(ssq_scr.shape, F32)

    def residual(rows):
        ms = ssq_scr[1 - cur, rows, 0:1] * (1.0 / (nj * tn))
        gate = gate_ref[rows, :] if gate_ref.shape[0] > 1 else gate_ref[...]
        y_prev = y_scr[j, rows, :].astype(F32)
        o_ref[rows, :] = x_ref[rows, :] + gate * (y_prev * lax.rsqrt(ms + EPS) * gpost_ref[...])

    k = a_ref.shape[1]
    n_k = 1
    while 2 * n_k <= min(KCHUNKS_DOWN, k // MXU_DEPTH) and tm % (2 * n_k * SUBLANES) == 0:
        n_k *= 2
    bounds = [k // MXU_DEPTH * c // n_k * MXU_DEPTH for c in range(n_k)] + [k]
    rows_per = tm // n_k

    @pl.when(row < ni)
    def _():
        y = None
        for c in range(n_k):
            ks = slice(bounds[c], bounds[c + 1])
            part = jnp.dot(a_ref[:, ks], b_ref[ks, :], preferred_element_type=F32)
            y = part if y is None else y + part
            residual(slice(c * rows_per, (c + 1) * rows_per))
        y_scr[j] = y.astype(y_scr.dtype)
        sq = jnp.broadcast_to(jnp.sum(y * y, axis=1, keepdims=True), (tm, LANES))
        ssq_scr[cur] = jnp.where(j == 0, sq, ssq_scr[cur] + sq)

    @pl.when(row == ni)
    def _():
        residual(slice(0, tm))


def _ffn_down_residual(a, b, x, gpost, gate):
    m, k = a.shape
    n = b.shape[1]
    tm = _pick(m, TM_DOWN)
    tn = _pick(n, TN_DOWN)
    ni, nj = m // tm, n // tn
    r_d = lambda s: jnp.minimum(s // nj, ni - 1)
    r_e = lambda s: jnp.maximum(s // nj - 1, 0)
    j_e = lambda s: jnp.where(s < nj, 0, s % nj)
    if gate.shape[0] == 1:
        gate_spec = pl.BlockSpec((1, tn), lambda s: (0, j_e(s)))
    else:
        gate_spec = pl.BlockSpec((tm, tn), lambda s: (r_e(s), j_e(s)))
    return pl.pallas_call(
        functools.partial(_ffn_down_kernel, nj=nj, ni=ni),
        grid=((ni + 1) * nj,),
        in_specs=[pl.BlockSpec((tm, k), lambda s: (r_d(s), 0)),
                  pl.BlockSpec((k, tn), lambda s: (0, jnp.where(s // nj < ni, s % nj, nj - 1))),
                  pl.BlockSpec((tm, tn), lambda s: (r_e(s), j_e(s))),
                  pl.BlockSpec((1, tn), lambda s: (0, j_e(s))),
                  gate_spec],
        out_specs=pl.BlockSpec((tm, tn), lambda s: (r_e(s), j_e(s))),
        out_shape=jax.ShapeDtypeStruct((m, n), F32),
        scratch_shapes=[pltpu.VMEM((nj, tm, tn), BF16), pltpu.VMEM((2, tm, LANES), F32)],
        compiler_params=_cparams("arbitrary", fuse_inputs=[False, True, False, False, False]),
        name="ffn_down",
    )(a, b, x, gpost, gate)


def _matmul_kernel(a_ref, b_ref, *o_refs):
    acc = jnp.dot(a_ref[...], b_ref[...], preferred_element_type=F32)
    for o_ref in o_refs:
        if len(o_ref.shape) == 3:
            for h in range(o_ref.shape[0]):
                o_ref[h] = acc[:, h * LANES:(h + 1) * LANES].astype(o_ref.dtype)
        else:
            o_ref[...] = acc.astype(o_ref.dtype)


def _matmul(a, b, col0, n, outs, tm_pref=None, tn_pref=None, name="matmul"):
    m, k = a.shape
    tm = _pick(m, TM_MM if tm_pref is None else tm_pref)
    tn = _pick(n, TN_MM if tn_pref is None else tn_pref)
    assert col0 % tn == 0 and tn % LANES == 0
    off = col0 // tn
    hpt = tn // LANES
    out_specs, out_shape = [], []
    for dt, head_major in outs:
        if head_major:
            out_specs.append(pl.BlockSpec((hpt, tm, LANES), lambda i, j: (j, i, 0)))
            out_shape.append(jax.ShapeDtypeStruct((n // LANES, m, LANES), dt))
        else:
            out_specs.append(pl.BlockSpec((tm, tn), lambda i, j: (i, j)))
            out_shape.append(jax.ShapeDtypeStruct((m, n), dt))
    return pl.pallas_call(
        _matmul_kernel,
        grid=(m // tm, n // tn),
        in_specs=[pl.BlockSpec((tm, k), lambda i, j: (i, 0)),
                  pl.BlockSpec((k, tn), lambda i, j: (0, j + off))],
        out_specs=out_specs,
        out_shape=out_shape,
        compiler_params=_cparams("parallel", "arbitrary", fuse_inputs=[False, True]),
        name=name,
    )(a, b)


def _cumsum_ones(cw):
    j = lax.broadcasted_iota(jnp.int32, (2 * cw, 2 * cw), 0)
    s = lax.broadcasted_iota(jnp.int32, (2 * cw, 2 * cw), 1)
    jj = jnp.where(j >= cw, j - cw, j)
    return jnp.where((s >= cw) | (jj > s), 1.0, 0.0).astype(BF16)


def _sb_tile(q, k, v, c, uu, mask_offset):
    return _sb_tiles([(q, k, v, c)], uu, mask_offset)[0]


def _sb_tiles(items, uu, mask_offset):
    scored = [_sb_scores(q, k, uu.shape[0] // 2, mask_offset) for q, k, _, _ in items]
    xs = [x for _, x in scored]
    r = jnp.dot(xs[0] if len(xs) == 1 else jnp.concatenate(xs, axis=0), uu, preferred_element_type=F32)
    outs, row = [], 0
    for (_, _, v, c), (chunks, x) in zip(items, scored):
        outs.append(_sb_weights(chunks, r[row:row + x.shape[0]], c, v))
        row += x.shape[0]
    return outs


def _sb_scores(q, k, cw, mask_offset):
    bq, dh = q.shape
    nc = k.shape[0] // cw
    z = lax.dot_general(q, k, (((1,), (1,)), ((), ())), preferred_element_type=F32) * (dh ** -0.5)
    chunks, xs = [], []
    for ci in range(nc):
        masked = mask_offset is not None and (ci + 1) * cw > mask_offset
        row0 = 0
        if masked:
            row0 = min(max(ci * cw - mask_offset + 1, 0), bq - BF16_ROWS) // BF16_ROWS * BF16_ROWS
        zc = z[row0:, ci * cw:(ci + 1) * cw]
        log_beta = jnp.minimum(zc, 0.0) - jnp.log(1.0 + jnp.exp(-jnp.abs(zc)))
        log_keep = log_beta - zc
        vis = None
        if masked:
            t_idx = lax.broadcasted_iota(jnp.int32, zc.shape, 0)
            s_idx = lax.broadcasted_iota(jnp.int32, zc.shape, 1)
            vis = s_idx < t_idx + (row0 + mask_offset - ci * cw)
            log_keep = jnp.where(vis, log_keep, 0.0)
        hi = log_keep.astype(BF16)
        lo = (log_keep - hi.astype(F32)).astype(BF16)
        chunks.append((log_beta, vis, row0))
        xs.append(jnp.concatenate([hi, lo], axis=1))
    return chunks, xs[0] if nc == 1 else jnp.concatenate(xs, axis=0)


def _sb_weights(chunks, r, c, v):
    cw = chunks[0][0].shape[1]
    ends, row = [], 0
    for log_beta, _, _ in chunks:
        row += log_beta.shape[0]
        ends.append(row)
    ws = [None] * len(chunks)
    for ci in reversed(range(len(chunks))):
        log_beta, vis, row0 = chunks[ci]
        rc = r[ends[ci] - log_beta.shape[0]:ends[ci]]
        w = jnp.exp(log_beta + rc[:, :cw] + c[row0:])
        if vis is not None:
            w = jnp.where(vis, w, 0.0)
        w = w.astype(BF16)
        c_new = c[row0:] + rc[:, cw:]
        if row0:
            w = jnp.concatenate([jnp.zeros((row0, cw), BF16), w], axis=0)
            c_new = jnp.concatenate([c[:row0], c_new], axis=0)
        ws[ci] = w
        c = c_new
    w = ws[0] if len(ws) == 1 else jnp.concatenate(ws, axis=1)
    return jnp.dot(w, v, preferred_element_type=F32), c


def _sb_prompt_kernel(q_ref, k_ref, v_ref, o_ref, acc_scr, c_scr, *, bq, n_par):
    n_blocks = q_ref.shape[0] // bq
    n_groups = (n_blocks - 1) // n_par
    uu = _cumsum_ones(LANES)
    c0 = jnp.zeros((bq, LANES), F32)

    acc, _ = _sb_tile(q_ref[0:bq, :], k_ref[0:bq, :], v_ref[0:bq, :], c0, uu, 0)
    o_ref[0:bq, :] = acc.astype(o_ref.dtype)

    def group(g, carry):
        items, q0s = [], []
        for u in range(n_par):
            q0 = pl.multiple_of((1 + g + u * n_groups) * bq, bq)
            k0 = pl.multiple_of(q0 - bq, bq)
            items.append((q_ref[pl.ds(q0, bq), :], k_ref[pl.ds(k0, 2 * bq), :], v_ref[pl.ds(k0, 2 * bq), :], c0))
            q0s.append(q0)
        res = _sb_tiles(items, uu, bq)
        for q0, (acc, _) in zip(q0s, res):
            o_ref[pl.ds(q0, bq), :] = acc.astype(o_ref.dtype)
        c_max = res[0][1]
        for _, c in res[1:]:
            c_max = jnp.maximum(c_max, c)

        @pl.when(jnp.max(c_max) > LOG_WEIGHT_ZERO)
        def _():
            for u, (acc, c) in enumerate(res):
                acc_scr[u] = acc
                c_scr[u] = c

            def older_keys(u, carry):
                qb = 1 + g + u * n_groups
                q0 = pl.multiple_of(qb * bq, bq)
                k0 = pl.multiple_of(q0 - bq, bq)
                q = q_ref[pl.ds(q0, bq), :]
                n_wide = (qb - 1) // 2

                def more(st):
                    j, _, c = st
                    return jnp.logical_and(j < n_wide, jnp.max(c) > LOG_WEIGHT_ZERO)

                def wide_tile(st):
                    j, acc, c = st
                    ks = pl.multiple_of(k0 - 2 * bq * (j + 1), bq)
                    da, c = _sb_tile(q, k_ref[pl.ds(ks, 2 * bq), :], v_ref[pl.ds(ks, 2 * bq), :], c, uu, None)
                    return j + 1, acc + da, c

                j, acc, c = lax.while_loop(more, wide_tile, (jnp.int32(0), acc_scr[u], c_scr[u]))
                need_last = jnp.logical_and(jnp.logical_and((qb - 1) % 2 == 1, j == n_wide),
                                            jnp.max(c) > LOG_WEIGHT_ZERO)

                def last_tile(acc, c):
                    da, _ = _sb_tile(q, k_ref[0:bq, :], v_ref[0:bq, :], c, uu, None)
                    return acc + da

                acc = lax.cond(need_last, last_tile, lambda acc, c: acc, acc, c)
                o_ref[pl.ds(q0, bq), :] = acc.astype(o_ref.dtype)
                return carry

            lax.fori_loop(0, n_par, older_keys, 0)

        return carry

    lax.fori_loop(0, n_groups, group, 0)


def _sb_prompt(q, k, v):
    h, t, dh = q.shape
    bq = _pick(t, BQ_ATT)
    n_rest = t // bq - 1
    n_par = next(u for u in (7, 4, 3, 2, 1) if n_rest % u == 0)
    spec = pl.BlockSpec((None, t, dh), lambda i: (i, 0, 0))
    return pl.pallas_call(
        functools.partial(_sb_prompt_kernel, bq=bq, n_par=n_par),
        grid=(h,),
        in_specs=[spec, spec, spec],
        out_specs=spec,
        out_shape=jax.ShapeDtypeStruct((h, t, dh), BF16),
        scratch_shapes=[pltpu.VMEM((n_par, bq, dh), F32), pltpu.VMEM((n_par, bq, LANES), F32)],
        compiler_params=_cparams("parallel"),
        name="sb_prompt",
    )(q, k, v)


def _sb_sample_kernel(q_ref, kn_ref, vn_ref, kp_ref, vp_ref, o_ref, acc_scr, c_scr, *, tk):
    n_heads, tq, dh = q_ref.shape
    kt = pl.program_id(1)
    uu = _cumsum_ones(LANES)

    @pl.when(kt == 0)
    def _():
        pad = jnp.zeros((LANES - tq, dh), BF16)
        c0 = jnp.zeros((tq, LANES), F32)
        items = [(q_ref[h], jnp.concatenate([kn_ref[h], pad], axis=0),
                  jnp.concatenate([vn_ref[h], pad], axis=0), c0) for h in range(n_heads)]
        for h, (acc, c) in enumerate(_sb_tiles(items, uu, 0)):
            acc_scr[h] = acc
            c_scr[h] = c

    @pl.when(jnp.max(c_scr[...]) > LOG_WEIGHT_ZERO)
    def _():
        items = [(q_ref[h], kp_ref[0, pl.ds(h, tk, stride=n_heads), :].astype(BF16),
                  vp_ref[0, pl.ds(h, tk, stride=n_heads), :].astype(BF16), c_scr[h]) for h in range(n_heads)]
        for h, (da, c) in enumerate(_sb_tiles(items, uu, None)):
            acc_scr[h] = acc_scr[h] + da
            c_scr[h] = c

    @pl.when(kt == pl.num_programs(1) - 1)
    def _():
        for h in range(n_heads):
            o_ref[h] = acc_scr[h].astype(o_ref.dtype)


def _sb_sample(q, kn, vn, kp, vp, tq):
    h, m, dh = q.shape
    bsz = m // tq
    p = kp.shape[1] // h
    tk = _pick(p, TK_SAMPLE)
    n_kt = p // tk
    assert tq <= LANES and tk % LANES == 0
    new = pl.BlockSpec((h, tq, dh), lambda b, kt: (0, b, 0))
    past = pl.BlockSpec((1, tk * h, dh), lambda b, kt: (b, n_kt - 1 - kt, 0))
    return pl.pallas_call(
        functools.partial(_sb_sample_kernel, tk=tk),
        grid=(bsz, n_kt),
        in_specs=[new, new, new, past, past],
        out_specs=new,
        out_shape=jax.ShapeDtypeStruct((h, m, dh), BF16),
        scratch_shapes=[pltpu.VMEM((h, tq, dh), F32), pltpu.VMEM((h, tq, LANES), F32)],
        compiler_params=_cparams("parallel", "arbitrary"),
        name="sb_sample",
    )(q, kn, vn, kp, vp)


def _shift_rows(x, prev8, s):
    n, c = x.shape
    groups = n // SUBLANES
    rot = pltpu.roll(x.reshape(groups, SUBLANES, c), s, 1)
    before = pltpu.roll(prev8, s, 0).reshape(1, SUBLANES, c)
    if groups > 1:
        before = jnp.concatenate([before, rot[:groups - 1]], axis=0)
    sub = lax.broadcasted_iota(jnp.int32, rot.shape, 1)
    return jnp.where(sub < s, before, rot).reshape(n, c)


def _causal_conv(x, prev8, w, b):
    width = w.shape[0]
    y = b + w[width - 1:width] * x
    for j in range(width - 1):
        y = y + w[j:j + 1] * _shift_rows(x, prev8, width - 1 - j)
    return y


def _linear_scan(a, u, h_in):
    n, c = a.shape
    a = a.reshape(n // SUBLANES, SUBLANES, c)
    u = u.reshape(n // SUBLANES, SUBLANES, c)
    sub = lax.broadcasted_iota(jnp.int32, a.shape, 1)
    d = 1
    while d < SUBLANES:
        keep = sub >= d
        a_sh = jnp.where(keep, pltpu.roll(a, d, 1), 1.0)
        u_sh = jnp.where(keep, pltpu.roll(u, d, 1), 0.0)
        u = u + a * u_sh
        a = a * a_sh
        d *= 2
    a = a.reshape(n, c)
    u = u.reshape(n, c)
    hs = []
    for g in range(n // SUBLANES):
        rows = slice(g * SUBLANES, (g + 1) * SUBLANES)
        hg = a[rows] * h_in + u[rows]
        hs.append(hg)
        h_in = hg[SUBLANES - 1:SUBLANES, :]
    return hs[0] if len(hs) == 1 else jnp.concatenate(hs, axis=0)


def _lru_kernel(xl_ref, gl_ref, prev_ref, h0_ref, cw_ref, cb_ref, wa_ref, ba_ref, wx_ref, bx_ref, lam_ref,
                o_ref, conv8_ref, h8_ref, prev_scr, h_scr, *, n_seq, piece):
    t = pl.program_id(1)

    @pl.when(t == 0)
    def _():
        prev_scr[...] = prev_ref[...]
        h_scr[...] = h0_ref[...]

    lam = lam_ref[...]
    softplus_neg_lam = jnp.maximum(-lam, 0.0) + jnp.log1p(jnp.exp(-jnp.abs(lam)))
    wa = wa_ref[0]
    wx = wx_ref[0]
    seq_rows = xl_ref.shape[0] // n_seq
    for s in range(n_seq):
        prev8 = prev_scr[s]
        h8 = h_scr[s]
        for pc in range(seq_rows // piece):
            rows = slice(s * seq_rows + pc * piece, s * seq_rows + (pc + 1) * piece)
            x = xl_ref[rows, :]
            xc = _causal_conv(x, prev8, cw_ref[...], cb_ref[...])
            xcb = xc.astype(BF16)
            r = jax.nn.sigmoid(jnp.dot(xcb, wa, preferred_element_type=F32) + ba_ref[...])
            i = jax.nn.sigmoid(jnp.dot(xcb, wx, preferred_element_type=F32) + bx_ref[...])
            log_a = -LRU_C * r * softplus_neg_lam
            a = jnp.exp(log_a)
            u = jnp.sqrt(-jnp.tanh(log_a) * (a * a + 1.0)) * (i * xc)
            hs = _linear_scan(a, u, h8[SUBLANES - 1:SUBLANES, :])
            o_ref[rows, :] = (hs * _gelu_tanh(gl_ref[rows, :])).astype(o_ref.dtype)
            prev8 = x[piece - SUBLANES:, :]
            h8 = hs[piece - SUBLANES:, :]
        prev_scr[s] = prev8
        h_scr[s] = h8
        conv8_ref[s] = prev8
        h8_ref[s] = h8


def _lru(xg, prev8, h08, conv_w, conv_b, w_a, b_a, w_x, b_x, lam, n_seq, seq_len):
    m, c2 = xg.shape
    c = c2 // 2
    nb = c // LANES
    assert w_a.shape == (nb, LANES, LANES) and seq_len % SUBLANES == 0 and seq_len >= conv_w.shape[0] - 1
    if n_seq == 1:
        tt = _pick(seq_len, TT_LRU)
        s_t, piece = 1, _pick(tt, PC_LRU)
    else:
        tt = m
        s_t, piece = n_seq, seq_len
    nt = m // tt
    width = conv_w.shape[0]
    vec = pl.BlockSpec((1, LANES), lambda n, t: (0, n))
    st8 = pl.BlockSpec((s_t, SUBLANES, LANES), lambda n, t: (0, 0, n))
    out8 = pl.BlockSpec((s_t, SUBLANES, LANES), lambda n, t: (t, 0, n))
    gate_w = pl.BlockSpec((1, LANES, LANES), lambda n, t: (n, 0, 0))
    return pl.pallas_call(
        functools.partial(_lru_kernel, n_seq=s_t, piece=piece),
        grid=(nb, nt),
        in_specs=[pl.BlockSpec((tt, LANES), lambda n, t: (t, n)),
                  pl.BlockSpec((tt, LANES), lambda n, t: (t, n + nb)),
                  st8, st8,
                  pl.BlockSpec((width, LANES), lambda n, t: (0, n)), vec,
                  gate_w, vec, gate_w, vec, vec],
        out_specs=[pl.BlockSpec((tt, LANES), lambda n, t: (t, n)), out8, out8],
        out_shape=[jax.ShapeDtypeStruct((m, c), BF16),
                   jax.ShapeDtypeStruct((nt * s_t, SUBLANES, c), F32),
                   jax.ShapeDtypeStruct((nt * s_t, SUBLANES, c), F32)],
        scratch_shapes=[pltpu.VMEM((s_t, SUBLANES, LANES), F32), pltpu.VMEM((s_t, SUBLANES, LANES), F32)],
        compiler_params=_cparams("parallel", "arbitrary"),
        name="rg_lru",
    )(xg, xg, prev8, h08, conv_w, conv_b, w_a, b_a, w_x, b_x, lam)


def _ffn_up_kernel(h_ref, wg_ref, wv_ref, pg_ref, pv_ref, cwg_ref, cwv_ref, cbg_ref, cbv_ref,
                   act_ref, lastg_ref, lastv_ref, raw_g, raw_v, carry_g, carry_v, *, n_seq, nj):
    s = pl.program_id(0)
    tile_e = jnp.maximum(s - 1, 0)
    i_e = lax.div(tile_e, nj)
    j_e = lax.rem(tile_e, nj)

    @pl.when(s == 0)
    def _():
        raw_g[1] = jnp.zeros(raw_g.shape[1:], F32)
        raw_v[1] = jnp.zeros(raw_v.shape[1:], F32)

    @pl.when(i_e == 0)
    def _():
        carry_g[j_e] = pg_ref[...]
        carry_v[j_e] = pv_ref[...]

    valid = s > 0
    tm = h_ref.shape[0]
    seq_rows = tm // n_seq
    piece = min(seq_rows, PIECE_FFN)

    def step(cur, prv):
        pieces = [(sq, p0) for sq in range(n_seq) for p0 in range(sq * seq_rows, (sq + 1) * seq_rows, piece)]
        n_k = max(1, min(KCHUNKS_FFN, len(pieces), h_ref.shape[1] // MXU_DEPTH))
        kc_size = h_ref.shape[1] // n_k
        prev_g = [carry_g[j_e, sq] for sq in range(n_seq)]
        prev_v = [carry_v[j_e, sq] for sq in range(n_seq)]
        old_g, old_v = list(prev_g), list(prev_v)
        acc_g = acc_v = None
        done = 0
        for kc in range(n_k):
            ks = slice(kc * kc_size, (kc + 1) * kc_size)
            h = h_ref[:, ks]
            part_g = jnp.dot(h, wg_ref[ks, :].astype(BF16), preferred_element_type=F32)
            part_v = jnp.dot(h, wv_ref[ks, :].astype(BF16), preferred_element_type=F32)
            acc_g = part_g if acc_g is None else acc_g + part_g
            acc_v = part_v if acc_v is None else acc_v + part_v
            upto = len(pieces) * (kc + 1) // n_k
            for sq, p0 in pieces[done:upto]:
                xg = raw_g[prv, p0:p0 + piece, :]
                xv = raw_v[prv, p0:p0 + piece, :]
                yg = _causal_conv(xg, prev_g[sq], cwg_ref[...], cbg_ref[...])
                yv = _causal_conv(xv, prev_v[sq], cwv_ref[...], cbv_ref[...])
                act_ref[p0:p0 + piece, :] = (_gelu_tanh(yg) * yv).astype(act_ref.dtype)
                prev_g[sq] = xg[piece - SUBLANES:, :]
                prev_v[sq] = xv[piece - SUBLANES:, :]
            done = upto
        raw_g[cur] = acc_g
        raw_v[cur] = acc_v
        for sq in range(n_seq):
            carry_g[j_e, sq] = jnp.where(valid, prev_g[sq], old_g[sq])
            carry_v[j_e, sq] = jnp.where(valid, prev_v[sq], old_v[sq])
            lastg_ref[sq] = prev_g[sq]
            lastv_ref[sq] = prev_v[sq]

    slot = lax.rem(s, 2)
    pl.when(slot == 0)(lambda: step(0, 1))
    pl.when(slot == 1)(lambda: step(1, 0))


def _ffn_up(h, w_up, prev8, conv_w, conv_b, n_seq, seq_len):
    m, d = h.shape
    f = w_up.shape[1] // 2
    tn = _pick(f, TN_FFN)
    assert seq_len % SUBLANES == 0 and seq_len >= conv_w.shape[0] - 1
    if n_seq == 1:
        tm = _pick(seq_len, TM_FFN)
        s_t = 1
    else:
        tm = m
        s_t = n_seq
    ni, nj = m // tm, f // tn
    n_tiles = ni * nj
    width = conv_w.shape[0]
    i_d = lambda s: jnp.minimum(s, n_tiles - 1) // nj
    j_d = lambda s: jnp.minimum(s, n_tiles - 1) % nj
    i_e = lambda s: jnp.maximum(s - 1, 0) // nj
    j_e = lambda s: jnp.maximum(s - 1, 0) % nj
    st8 = lambda off: pl.BlockSpec((s_t, SUBLANES, tn), lambda s: (0, 0, j_e(s) + off))
    out8 = pl.BlockSpec((s_t, SUBLANES, tn), lambda s: (i_e(s), 0, j_e(s)))
    wspec = lambda off: pl.BlockSpec((d, tn), lambda s: (0, j_d(s) + off))
    cw = lambda off: pl.BlockSpec((width, tn), lambda s: (0, j_e(s) + off))
    cb = lambda off: pl.BlockSpec((1, tn), lambda s: (0, j_e(s) + off))
    return pl.pallas_call(
        functools.partial(_ffn_up_kernel, n_seq=s_t, nj=nj),
        grid=(n_tiles + 1,),
        in_specs=[pl.BlockSpec((tm, d), lambda s: (i_d(s), 0)),
                  wspec(0), wspec(nj), st8(0), st8(nj), cw(0), cw(nj), cb(0), cb(nj)],
        out_specs=[pl.BlockSpec((tm, tn), lambda s: (i_e(s), j_e(s))), out8, out8],
        out_shape=[jax.ShapeDtypeStruct((m, f), BF16),
                   jax.ShapeDtypeStruct((ni * s_t, SUBLANES, f), F32),
                   jax.ShapeDtypeStruct((ni * s_t, SUBLANES, f), F32)],
        scratch_shapes=[pltpu.VMEM((2, tm, tn), F32), pltpu.VMEM((2, tm, tn), F32),
                        pltpu.VMEM((nj, s_t, SUBLANES, tn), F32), pltpu.VMEM((nj, s_t, SUBLANES, tn), F32)],
        compiler_params=_cparams("arbitrary"),
        name="ffn_up",
    )(h, w_up, w_up, prev8, prev8, conv_w, conv_w, conv_b, conv_b)


def _state8(state, n_rows):
    return jnp.pad(state.astype(F32), ((0, 0), (SUBLANES - n_rows, 0), (0, 0)))


def _layer(x, mods, p, past, n_seq, seq_len):
    m, d = x.shape
    d_sb = d // 2
    d_lru = d // 2
    sh_m, sc_m, g_m, sh_f, sc_f, g_f = mods
    k_past, v_past, conv_lru_prev, h_prev, conv_ffn_prev = past

    h = _norm_mod(x, p['g_pre_mix'], sc_m, sh_m)
    (q,) = _matmul(h, p['w_in'], 0, d_sb, [(BF16, True)], name="proj_q")
    k32, k16 = _matmul(h, p['w_in'], d_sb, d_sb, [(F32, False), (BF16, True)], name="proj_k")
    v32, v16 = _matmul(h, p['w_in'], 2 * d_sb, d_sb, [(F32, False), (BF16, True)], name="proj_v")
    (xg,) = _matmul(h, p['w_in'], 3 * d_sb, 2 * d_lru, [(F32, False)], name="proj_lru")

    if k_past is None:
        o_sb = _sb_prompt(q, k16, v16)
    else:
        o_sb = _sb_sample(q, k16, v16, k_past, v_past, seq_len)

    w_c = p['w_conv_lru']
    o_lru, conv8, h8 = _lru(xg, _state8(conv_lru_prev, w_c.shape[0] - 1),
                            jnp.broadcast_to(h_prev.astype(F32)[:, None, :], (n_seq, SUBLANES, d_lru)),
                            w_c, p['b_conv_lru'], p['w_gate_a'], p['b_gate_a'], p['w_gate_x'], p['b_gate_x'],
                            p['lru_lambda'], n_seq, seq_len)
    conv_lru_new = conv8[-n_seq:, SUBLANES - (w_c.shape[0] - 1):, :]
    h_last = h8[-n_seq:, SUBLANES - 1, :]

    mix_in = _group_norm_concat(o_sb, o_lru, p['g_grp_attn'], p['g_grp_lru'])
    (mixed,) = _matmul(mix_in, p['w_out'], 0, d, [(BF16, False)], name="proj_out")
    x1, h2 = _post_mix(x, mixed, p['g_post_mix'], g_m, p['g_pre_ffn'], sc_f, sh_f)

    w_f = p['w_conv_ffn']
    act, last_g, last_v = _ffn_up(h2, p['w_up'], _state8(conv_ffn_prev, w_f.shape[0] - 1),
                                  w_f, p['b_conv_ffn'], n_seq, seq_len)
    keep = SUBLANES - (w_f.shape[0] - 1)
    conv_ffn_new = jnp.concatenate([last_g[-n_seq:, keep:, :], last_v[-n_seq:, keep:, :]], axis=-1)
    out = _ffn_down_residual(act, p['w_down'], x1, p['g_post_ffn'], g_f)
    return out, (k32, v32, conv_lru_new, h_last, conv_ffn_new)


def kernel(x_prompt, x_sample, c_prompt, c_sample, cache_k, cache_v, state_conv_lru, state_lru, state_conv_ffn, w_ada, b_ada, g_pre_mix, w_in, w_conv_lru, b_conv_lru, w_gate_a, b_gate_a, w_gate_x, b_gate_x, lru_lambda, g_grp_attn, g_grp_lru, w_out, g_post_mix, g_pre_ffn, w_up, w_conv_ffn, b_conv_ffn, w_down, g_post_ffn):
    depth = w_ada.shape[0]
    bp, tp, d = x_prompt.shape
    bs, ts, _ = x_sample.shape
    assert bp == 1, "the prompt kernels treat the prompt as one sequence"
    d_sb = d // 2
    n_heads = d_sb // HEAD_DIM
    n_mod = w_ada.shape[2] // d

    xp = x_prompt.reshape(bp * tp, d)
    xs = x_sample.reshape(bs * ts, d)
    n_c = bp + bs
    n_c_pad = -(-n_c // SUBLANES) * SUBLANES
    c_all = jnp.concatenate([c_prompt, c_sample, jnp.zeros((n_c_pad - n_c, d), F32)], axis=0)

    st_p = [[], [], [], [], []]
    st_s = [[], [], [], [], []]
    for l in range(depth):
        row = lambda a: a[l][None, :]
        p = {
            'g_pre_mix': row(g_pre_mix), 'w_in': w_in[l].astype(BF16),
            'w_conv_lru': w_conv_lru[l], 'b_conv_lru': row(b_conv_lru),
            'w_gate_a': w_gate_a[l].astype(BF16), 'b_gate_a': row(b_gate_a),
            'w_gate_x': w_gate_x[l].astype(BF16), 'b_gate_x': row(b_gate_x),
            'lru_lambda': row(lru_lambda), 'g_grp_attn': row(g_grp_attn), 'g_grp_lru': row(g_grp_lru),
            'w_out': w_out[l].astype(BF16), 'g_post_mix': row(g_post_mix), 'g_pre_ffn': row(g_pre_ffn),
            'w_up': w_up[l], 'w_conv_ffn': w_conv_ffn[l], 'b_conv_ffn': row(b_conv_ffn),
            'w_down': w_down[l].astype(BF16), 'g_post_ffn': row(g_post_ffn),
        }
        mod = _ada(c_all, w_ada[l], row(b_ada))
        mods_p = [mod[0:1, i * d:(i + 1) * d] for i in range(n_mod)]
        mods_s = [jnp.repeat(mod[bp:bp + bs, i * d:(i + 1) * d], ts, axis=0) for i in range(n_mod)]

        d_lru = d // 2
        f2 = w_up.shape[2]
        prompt_past = (None, None, jnp.zeros((bp, w_conv_lru.shape[1] - 1, d_lru), F32),
                       jnp.zeros((bp, d_lru), F32), jnp.zeros((bp, w_conv_ffn.shape[1] - 1, f2), F32))
        xp, new_p = _layer(xp, mods_p, p, prompt_past, bp, tp)
        past_len = cache_k.shape[2]
        sample_past = (cache_k[l].reshape(bs, past_len * n_heads, HEAD_DIM),
                       cache_v[l].reshape(bs, past_len * n_heads, HEAD_DIM),
                       state_conv_lru[l], state_lru[l], state_conv_ffn[l])
        xs, new_s = _layer(xs, mods_s, p, sample_past, bs, ts)
        for j in range(5):
            st_p[j].append(new_p[j])
            st_s[j].append(new_s[j])

    def pack(st, b, t):
        k = jnp.stack(st[0], axis=0).reshape(depth, b, t, n_heads, HEAD_DIM)
        v = jnp.stack(st[1], axis=0).reshape(depth, b, t, n_heads, HEAD_DIM)
        return k, v, jnp.stack(st[2], axis=0), jnp.stack(st[3], axis=0), jnp.stack(st[4], axis=0)

    k_p, v_p, cl_p, h_p, cf_p = pack(st_p, bp, tp)
    k_s, v_s, cl_s, h_s, cf_s = pack(st_s, bs, ts)
    return (xp.reshape(bp, tp, d), xs.reshape(bs, ts, d), k_p, v_p, cl_p, h_p, cf_p, k_s, v_s, cl_s, h_s, cf_s)
```
